```python
import jax, jax.numpy as jnp
from jax import lax
import numpy as np

D_MODEL = 2048
BATCH = 4
SEQ = 4096
DEPTH = 2

N_EVEN = (DEPTH + 1) // 2
N_ODD = DEPTH // 2
EPS = 1e-6

CONV_DIM = 1024
CONV_WIDTH = 3
MLA_HEADS = 8
Q_LORA = 512
KV_LORA = 256
QK_NOPE = 128
QK_ROPE = 64
V_HEAD = 128
ROPE_THETA = 10000.0
Q_BLOCK = 128
IN_COLS = 3 * CONV_DIM + Q_LORA + KV_LORA + QK_ROPE
MIX_OUT = CONV_DIM + MLA_HEADS * V_HEAD
POOL_WINDOWS = (2, 4, 8, 16)
POOL_GROUPS = 4
POOL_DIM = D_MODEL // POOL_GROUPS
D_FF_DENSE = 5632
N_EXPERTS = 8
TOP_K = 2
D_FF_EXPERT = 2816

kernel_name = "hybrid_conv_mla_pool_moe_block"


def rmsnorm(x, g):
    xf = x.astype(jnp.float32)
    y = xf * lax.rsqrt(jnp.mean(xf * xf, axis=-1, keepdims=True) + EPS)
    return (y * g.astype(jnp.float32)).astype(x.dtype)


def swiglu(h, w_gate, w_up, w_down):
    return (jax.nn.silu(h @ w_gate) * (h @ w_up)) @ w_down


def causal_short_conv(u, w):
    S = u.shape[1]
    up = jnp.pad(u, ((0, 0), (CONV_WIDTH - 1, 0), (0, 0)))
    return sum(w[j] * up[:, j:j + S] for j in range(CONV_WIDTH))


def rope(t, cos, sin):
    half = t.shape[-1] // 2
    tf = t.astype(jnp.float32)
    t1, t2 = tf[..., :half], tf[..., half:]
    return jnp.concatenate([t1 * cos - t2 * sin, t1 * sin + t2 * cos], axis=-1).astype(t.dtype)


def causal_block_attention(q_nope, q_rope, k_nope, k_rope, v):
    B, S, H, _ = q_nope.shape
    nb = S // Q_BLOCK
    scale = (QK_NOPE + QK_ROPE) ** -0.5
    key_pos = jnp.arange(S)

    def to_blocks(t):
        return jnp.moveaxis(t.reshape(B, nb, Q_BLOCK, *t.shape[2:]), 1, 0)

    def one_block(args):
        qn, qr, start = args
        s = (jnp.einsum('bqhd,bkhd->bhqk', qn, k_nope, preferred_element_type=jnp.float32)
             + jnp.einsum('bqhd,bkd->bhqk', qr, k_rope, preferred_element_type=jnp.float32))
        q_pos = start + jnp.arange(Q_BLOCK)
        mask = key_pos[None, :] <= q_pos[:, None]
        s = jnp.where(mask, s * scale, -jnp.inf)
        p = jax.nn.softmax(s, axis=-1).astype(v.dtype)
        return jnp.einsum('bhqk,bkhd->bqhd', p, v)

    starts = jnp.arange(nb) * Q_BLOCK
    out = lax.map(one_block, (to_blocks(q_nope), to_blocks(q_rope), starts))
    return jnp.moveaxis(out, 0, 1).reshape(B, S, H * V_HEAD)


def conv_mla_mixer(h, cos, sin, w_in, conv_w, q_norm, w_uq, kv_norm, w_ukv, w_out):
    B, S, _ = h.shape
    proj = h @ w_in
    o = np.cumsum([0, CONV_DIM, CONV_DIM, CONV_DIM, Q_LORA, KV_LORA, QK_ROPE])
    x_in, gate_b, gate_c, c_q, c_kv, k_r = [proj[..., o[i]:o[i + 1]] for i in range(6)]
    y_a = gate_b * causal_short_conv(gate_c * x_in, conv_w)
    q = (rmsnorm(c_q, q_norm) @ w_uq).reshape(B, S, MLA_HEADS, QK_NOPE + QK_ROPE)
    kv = (rmsnorm(c_kv, kv_norm) @ w_ukv).reshape(B, S, MLA_HEADS, QK_NOPE + V_HEAD)
    q_nope, q_rope = q[..., :QK_NOPE], rope(q[..., QK_NOPE:], cos[:, None, :], sin[:, None, :])
    k_nope, v = kv[..., :QK_NOPE], kv[..., QK_NOPE:]
    k_rope = rope(k_r, cos, sin)
    y_b = causal_block_attention(q_nope, q_rope, k_nope, k_rope, v)
    return jnp.concatenate([y_a, y_b], axis=-1) @ w_out


def multiscale_pool_mixer(h, pool_w, pool_scale):
    B, S, D = h.shape
    hg = h.reshape(B, S, POOL_GROUPS, POOL_DIM).astype(jnp.float32)
    cz = jnp.concatenate([jnp.zeros((B, 1, POOL_GROUPS, POOL_DIM), jnp.float32),
                          jnp.cumsum(hg, axis=1)], axis=1)
    t = jnp.arange(S)
    means = []
    for g, w in enumerate(POOL_WINDOWS):
        c = cz[:, :, g]
        upper = c[:, 1:]
        lower = jnp.concatenate([jnp.zeros((B, w - 1, POOL_DIM), jnp.float32), c[:, :S - w + 1]], axis=1)
        count = jnp.minimum(t + 1, w).astype(jnp.float32)[None, :, None]
        means.append((upper - lower) / count)
    pooled = (jnp.stack(means, axis=2) - hg).astype(h.dtype)
    y = jnp.einsum('bsgc,gcd->bsgd', pooled, pool_w).reshape(B, S, D)
    return y * pool_scale


def moe_swiglu(h, router_w, w_gate, w_up, w_down):
    logits = jnp.einsum('bsd,de->bse', h, router_w, preferred_element_type=jnp.float32)
    top_v, top_i = lax.top_k(logits, TOP_K)
    top_p = jax.nn.softmax(top_v, axis=-1)
    gates = jnp.sum(jax.nn.one_hot(top_i, N_EXPERTS, dtype=jnp.float32) * top_p[..., None],
                    axis=-2).astype(h.dtype)
    y = jnp.zeros_like(h)
    for e in range(N_EXPERTS):
        y = y + gates[..., e:e + 1] * swiglu(h, w_gate[e], w_up[e], w_down[e])
    return y


def setup_inputs(seed: int = 0) -> dict:
    key = jax.random.key(seed)
    ks = jax.random.split(key, 24)

    def w(k, shape, fan_in):
        return jax.random.normal(k, shape, jnp.float32) * fan_in ** -0.5

    def gain(k, shape):
        return 1.0 + 0.02 * jax.random.normal(k, shape, jnp.float32)

    E, O = N_EVEN, N_ODD
    return {
        "x": jax.random.normal(ks[0], (BATCH, SEQ, D_MODEL), jnp.float32),
        "norm_mix0": gain(ks[1], (E, D_MODEL)),
        "w_in": w(ks[2], (E, D_MODEL, IN_COLS), D_MODEL),
        "conv_w": w(ks[3], (E, CONV_WIDTH, CONV_DIM), CONV_WIDTH),
        "q_norm": gain(ks[4], (E, Q_LORA)),
        "w_uq": w(ks[5], (E, Q_LORA, MLA_HEADS * (QK_NOPE + QK_ROPE)), Q_LORA),
        "kv_norm": gain(ks[6], (E, KV_LORA)),
        "w_ukv": w(ks[7], (E, KV_LORA, MLA_HEADS * (QK_NOPE + V_HEAD)), KV_LORA),
        "w_out": w(ks[8], (E, MIX_OUT, D_MODEL), MIX_OUT),
        "norm_ffn0": gain(ks[9], (E, D_MODEL)),
        "ffn_w_gate": w(ks[10], (E, D_MODEL, D_FF_DENSE), D_MODEL),
        "ffn_w_up": w(ks[11], (E, D_MODEL, D_FF_DENSE), D_MODEL),
        "ffn_w_down": w(ks[12], (E, D_FF_DENSE, D_MODEL), D_FF_DENSE),
        "norm_mix1": gain(ks[13], (O, D_MODEL)),
        "pool_w": w(ks[14], (O, POOL_GROUPS, POOL_DIM, POOL_DIM), POOL_DIM),
        "pool_scale": gain(ks[15], (O, D_MODEL)),
        "norm_ffn1": gain(ks[16], (O, D_MODEL)),
        "router_w": w(ks[17], (O, D_MODEL, N_EXPERTS), D_MODEL),
        "moe_w_gate": w(ks[18], (O, N_EXPERTS, D_MODEL, D_FF_EXPERT), D_MODEL),
        "moe_w_up": w(ks[19], (O, N_EXPERTS, D_MODEL, D_FF_EXPERT), D_MODEL),
        "moe_w_down": w(ks[20], (O, N_EXPERTS, D_FF_EXPERT, D_MODEL), D_FF_EXPERT),
        "final_norm": gain(ks[21], (D_MODEL,)),
    }


def reference(x, norm_mix0, w_in, conv_w, q_norm, w_uq, kv_norm, w_ukv, w_out,
              norm_ffn0, ffn_w_gate, ffn_w_up, ffn_w_down,
              norm_mix1, pool_w, pool_scale, norm_ffn1, router_w,
              moe_w_gate, moe_w_up, moe_w_down, final_norm):
    S = x.shape[1]
    pos = jnp.arange(S, dtype=jnp.float32)
    inv_freq = ROPE_THETA ** (-jnp.arange(0, QK_ROPE, 2, dtype=jnp.float32) / QK_ROPE)
    ang = pos[:, None] * inv_freq[None, :]
    cos, sin = jnp.cos(ang), jnp.sin(ang)

    for layer in range(DEPTH):
        i = layer // 2
        if layer % 2 == 0:
            h = rmsnorm(x, norm_mix0[i])
            x = x + conv_mla_mixer(h, cos, sin, w_in[i], conv_w[i], q_norm[i], w_uq[i],
                                   kv_norm[i], w_ukv[i], w_out[i])
            h = rmsnorm(x, norm_ffn0[i])
            x = x + swiglu(h, ffn_w_gate[i], ffn_w_up[i], ffn_w_down[i])
        else:
            h = rmsnorm(x, norm_mix1[i])
            x = x + multiscale_pool_mixer(h, pool_w[i], pool_scale[i])
            h = rmsnorm(x, norm_ffn1[i])
            x = x + moe_swiglu(h, router_w[i], moe_w_gate[i], moe_w_up[i], moe_w_down[i])
    return rmsnorm(x, final_norm)
```

```python
import functools

import jax
import jax.numpy as jnp
from jax import lax
from jax.experimental import pallas as pl
from jax.experimental.pallas import tpu as pltpu

F32 = jnp.float32
BF16 = jnp.bfloat16

EPS = 1e-6
CONV_DIM = 1024
MLA_HEADS = 8
Q_LORA = 512
KV_LORA = 256
QK_NOPE = 128
QK_ROPE = 64
V_HEAD = 128
ROPE_THETA = 10000.0
POOL_WINDOWS = (2, 4, 8, 16)
N_EXPERTS = 8

LANES = 128
HEAD_PAD = 256
IN_COLS_PAD = 4096
LAT_OFF = 3 * CONV_DIM
HALO = 16
VMEM_LIMIT = 56 * 1024 * 1024


def _cparams(sem):
    return pltpu.CompilerParams(dimension_semantics=sem, vmem_limit_bytes=VMEM_LIMIT)


def _rms(x, g):
    return x * lax.rsqrt(jnp.mean(x * x, axis=-1, keepdims=True) + EPS) * g


def _norm_inproj_kernel(x_ref, g_ref, w_ref, o_ref, h_ref):
    @pl.when(pl.program_id(1) == 0)
    def _():
        h_ref[...] = _rms(x_ref[...], g_ref[...]).astype(BF16)

    o_ref[...] = jnp.dot(h_ref[...], w_ref[...], preferred_element_type=F32).astype(o_ref.dtype)


def _norm_inproj(x, g, w, *, tm=1024, tn=1024):
    T, D = x.shape
    N = w.shape[1]
    return pl.pallas_call(
        _norm_inproj_kernel,
        grid=(T // tm, N // tn),
        in_specs=[
            pl.BlockSpec((tm, D), lambda i, j: (i, 0)),
            pl.BlockSpec((1, D), lambda i, j: (0, 0)),
            pl.BlockSpec((D, tn), lambda i, j: (0, j)),
        ],
        out_specs=pl.BlockSpec((tm, tn), lambda i, j: (i, j)),
        out_shape=jax.ShapeDtypeStruct((T, N), BF16),
        scratch_shapes=[pltpu.VMEM((tm, D), BF16)],
        compiler_params=_cparams(("parallel", "arbitrary")),
        name="norm_inproj",
    )(x, g, w)


def _conv_kernel(xin_ref, gb_ref, gc_ref, hx_ref, hc_ref, w_ref, o_ref, *, ts, seq):
    i = pl.program_id(0)
    u = gc_ref[...].astype(F32) * xin_ref[...].astype(F32)
    hu = hc_ref[...].astype(F32) * hx_ref[...].astype(F32)
    hu = jnp.where((i * ts) % seq == 0, 0.0, hu)
    hm1 = hu[HALO - 1:HALO, :]
    hm2 = hu[HALO - 2:HALO - 1, :]
    row = lax.broadcasted_iota(jnp.int32, u.shape, 0)
    u1 = jnp.where(row == 0, hm1, pltpu.roll(u, 1, 0))
    u2 = jnp.where(row == 0, hm2, jnp.where(row == 1, hm1, pltpu.roll(u, 2, 0)))
    w = w_ref[...]
    y = w[0:1, :] * u2 + w[1:2, :] * u1 + w[2:3, :] * u
    o_ref[...] = (gb_ref[...].astype(F32) * y).astype(o_ref.dtype)


def _gated_conv(proj, conv_w, *, seq, ts=512):
    T = proj.shape[0]
    C = CONV_DIM
    hb = ts // HALO
    halo_map = lambda c: (lambda i: (jnp.maximum(i * hb - 1, 0), c))
    return pl.pallas_call(
        functools.partial(_conv_kernel, ts=ts, seq=seq),
        grid=(T // ts,),
        in_specs=[
            pl.BlockSpec((ts, C), lambda i: (i, 0)),
            pl.BlockSpec((ts, C), lambda i: (i, 1)),
            pl.BlockSpec((ts, C), lambda i: (i, 2)),
            pl.BlockSpec((HALO, C), halo_map(0)),
            pl.BlockSpec((HALO, C), halo_map(2)),
            pl.BlockSpec(conv_w.shape, lambda i: (0, 0)),
        ],
        out_specs=pl.BlockSpec((ts, C), lambda i: (i, 0)),
        out_shape=jax.ShapeDtypeStruct((T, C), BF16),
        compiler_params=_cparams(("parallel",)),
        name="gated_conv",
    )(proj, proj, proj, proj, proj, conv_w)


def _rope128(blk, cos_f, sin_a, sin_b):
    return blk * cos_f + pltpu.roll(blk, 96, 1) * sin_a + pltpu.roll(blk, 32, 1) * sin_b


def _qkv_kernel(lat_ref, qn_ref, kvn_ref, wq_ref, wk_ref, wv_ref, cos_ref, sa_ref, sb_ref,
                q_ref, k_ref, v_ref, *, scale):
    lat = lat_ref[...].astype(F32)
    cq = _rms(lat[:, :Q_LORA], qn_ref[...]).astype(BF16)
    ckv = _rms(lat[:, Q_LORA:Q_LORA + KV_LORA], kvn_ref[...]).astype(BF16)
    kr = lat[:, Q_LORA + KV_LORA:Q_LORA + KV_LORA + LANES]
    cos_f, sin_a, sin_b = cos_ref[...], sa_ref[...], sb_ref[...]
    q = jnp.dot(cq, wq_ref[...], preferred_element_type=F32) * scale
    kn = jnp.dot(ckv, wk_ref[...], preferred_element_type=F32)
    v_ref[...] = jnp.dot(ckv, wv_ref[...], preferred_element_type=F32).astype(v_ref.dtype)
    kr_rot = _rope128(kr, cos_f, sin_a, sin_b).astype(k_ref.dtype)
    for h in range(MLA_HEADS):
        o = h * HEAD_PAD
        q_ref[:, o:o + QK_NOPE] = q[:, o:o + QK_NOPE].astype(q_ref.dtype)
        q_ref[:, o + QK_NOPE:o + HEAD_PAD] = _rope128(
            q[:, o + QK_NOPE:o + HEAD_PAD], cos_f, sin_a, sin_b).astype(q_ref.dtype)
        k_ref[:, o:o + QK_NOPE] = kn[:, h * QK_NOPE:(h + 1) * QK_NOPE].astype(k_ref.dtype)
        k_ref[:, o + QK_NOPE:o + HEAD_PAD] = kr_rot


def _qkv_proj(proj, q_norm, kv_norm, wq, wk, wv, cos_f, sin_a, sin_b, *, seq, tm=512):
    T = proj.shape[0]
    H = MLA_HEADS
    lat_blk = LAT_OFF // 1024
    nseq = seq // tm
    tab = pl.BlockSpec((tm, LANES), lambda i: (i % nseq, 0))
    full = lambda a: pl.BlockSpec(a.shape, lambda i: (0, 0))
    return pl.pallas_call(
        functools.partial(_qkv_kernel, scale=float((QK_NOPE + QK_ROPE) ** -0.5)),
        grid=(T // tm,),
        in_specs=[
            pl.BlockSpec((tm, 1024), lambda i: (i, lat_blk)),
            full(q_norm), full(kv_norm), full(wq), full(wk), full(wv),
            tab, tab, tab,
        ],
        out_specs=[
            pl.BlockSpec((tm, H * HEAD_PAD), lambda i: (i, 0)),
            pl.BlockSpec((tm, H * HEAD_PAD), lambda i: (i, 0)),
            pl.BlockSpec((tm, H * V_HEAD), lambda i: (i, 0)),
        ],
        out_shape=[
            jax.ShapeDtypeStruct((T, H * HEAD_PAD), BF16),
            jax.ShapeDtypeStruct((T, H * HEAD_PAD), BF16),
            jax.ShapeDtypeStruct((T, H * V_HEAD), BF16),
        ],
        compiler_params=_cparams(("parallel",)),
        name="qkv_proj",
    )(proj, q_norm, kv_norm, wq, wk, wv, cos_f, sin_a, sin_b)


def _flash_kernel(q_ref, k_ref, v_ref, o_ref, m_ref, l_ref, acc_ref, *, tq):
    qi = pl.program_id(2)
    q = q_ref[...]
    m_ref[...] = jnp.full(m_ref.shape, -jnp.inf, F32)
    l_ref[...] = jnp.zeros(l_ref.shape, F32)
    acc_ref[...] = jnp.zeros(acc_ref.shape, F32)

    def block(j, masked):
        start = pl.multiple_of(j * tq, tq)
        k = k_ref[pl.ds(start, tq), :]
        v = v_ref[pl.ds(start, tq), :]
        s = lax.dot_general(q, k, (((1,), (1,)), ((), ())), preferred_element_type=F32)
        if masked:
            row = lax.broadcasted_iota(jnp.int32, s.shape, 0)
            col = lax.broadcasted_iota(jnp.int32, s.shape, 1)
            s = jnp.where(col <= row, s, -jnp.inf)
        m_prev = m_ref[...]
        m_new = jnp.maximum(m_prev, jnp.max(s, axis=-1, keepdims=True))
        alpha = jnp.exp(m_prev - m_new)
        p = jnp.exp(s - m_new)
        l_ref[...] = alpha * l_ref[...] + jnp.sum(p, axis=-1, keepdims=True)
        acc_ref[...] = alpha * acc_ref[...] + jnp.dot(p.astype(v.dtype), v, preferred_element_type=F32)
        m_ref[...] = m_new

    def body(j, c):
        block(j, False)
        return c

    lax.fori_loop(0, qi, body, 0)
    block(qi, True)
    o_ref[...] = (acc_ref[...] / l_ref[...]).astype(o_ref.dtype)


def _flash_attention(q, k, v, *, batch, seq, tq=512):
    T = q.shape[0]
    H = MLA_HEADS
    nq = seq // tq
    return pl.pallas_call(
        functools.partial(_flash_kernel, tq=tq),
        grid=(batch, H, nq),
        in_specs=[
            pl.BlockSpec((tq, HEAD_PAD), lambda b, h, i: (b * nq + i, h)),
            pl.BlockSpec((seq, HEAD_PAD), lambda b, h, i: (b, h)),
            pl.BlockSpec((seq, V_HEAD), lambda b, h, i: (b, h)),
        ],
        out_specs=pl.BlockSpec((tq, V_HEAD), lambda b, h, i: (b * nq + i, h)),
        out_shape=jax.ShapeDtypeStruct((T, H * V_HEAD), BF16),
        scratch_shapes=[
            pltpu.VMEM((tq, 1), F32),
            pltpu.VMEM((tq, 1), F32),
            pltpu.VMEM((tq, V_HEAD), F32),
        ],
        compiler_params=_cparams(("parallel", "parallel", "arbitrary")),
        name="flash_attention",
    )(q, k, v)


def _outproj_kernel(ya_ref, yb_ref, wa_ref, wb_ref, x_ref, g_ref, x1_ref, h1_ref):
    acc = jnp.dot(ya_ref[...], wa_ref[...], preferred_element_type=F32)
    acc = acc + jnp.dot(yb_ref[...], wb_ref[...], preferred_element_type=F32)
    x1 = x_ref[...] + acc
    x1_ref[...] = x1
    h1_ref[...] = _rms(x1, g_ref[...]).astype(h1_ref.dtype)


def _out_proj(ya, yb, wa, wb, x, g, *, tm=512):
    T, D = x.shape
    full = lambda a: pl.BlockSpec(a.shape, lambda i: (0, 0))
    return pl.pallas_call(
        _outproj_kernel,
        grid=(T // tm,),
        in_specs=[
            pl.BlockSpec((tm, ya.shape[1]), lambda i: (i, 0)),
            pl.BlockSpec((tm, yb.shape[1]), lambda i: (i, 0)),
            full(wa), full(wb),
            pl.BlockSpec((tm, D), lambda i: (i, 0)),
            full(g),
        ],
        out_specs=[pl.BlockSpec((tm, D), lambda i: (i, 0)), pl.BlockSpec((tm, D), lambda i: (i, 0))],
        out_shape=[jax.ShapeDtypeStruct((T, D), F32), jax.ShapeDtypeStruct((T, D), BF16)],
        compiler_params=_cparams(("parallel",)),
        name="out_proj",
    )(ya, yb, wa, wb, x, g)


def _ffn_kernel(h_ref, wg_ref, wu_ref, wd_ref, x_ref, o_ref):
    k = pl.program_id(1)

    @pl.when(k == 0)
    def _():
        o_ref[...] = x_ref[...]

    h = h_ref[...]
    g = jnp.dot(h, wg_ref[...], preferred_element_type=F32)
    u = jnp.dot(h, wu_ref[...], preferred_element_type=F32)
    a = (g * jax.nn.sigmoid(g) * u).astype(h.dtype)
    o_ref[...] += jnp.dot(a, wd_ref[...], preferred_element_type=F32)


def _dense_ffn(h, wg, wu, wd, x, *, tm=512, tf=512):
    T, D = x.shape
    F = wg.shape[1]
    return pl.pallas_call(
        _ffn_kernel,
        grid=(T // tm, F // tf),
        in_specs=[
            pl.BlockSpec((tm, D), lambda i, k: (i, 0)),
            pl.BlockSpec((D, tf), lambda i, k: (0, k)),
            pl.BlockSpec((D, tf), lambda i, k: (0, k)),
            pl.BlockSpec((tf, D), lambda i, k: (k, 0)),
            pl.BlockSpec((tm, D), lambda i, k: (i, 0)),
        ],
        out_specs=pl.BlockSpec((tm, D), lambda i, k: (i, 0)),
        out_shape=jax.ShapeDtypeStruct((T, D), F32),
        compiler_params=_cparams(("parallel", "arbitrary")),
        name="dense_ffn",
    )(h, wg, wu, wd, x)


def _pool_router_kernel(x_ref, halo_ref, g1_ref, pw_ref, ps_ref, g2_ref, rw_ref,
                        x3_ref, h3_ref, gates_ref, sel_ref, *, tm, seq):
    i = pl.program_id(0)
    x = x_ref[...]
    g1 = g1_ref[...]
    h = _rms(x, g1)
    hh = _rms(halo_ref[...], g1)
    t0 = (i * tm) % seq
    hh = jnp.where(t0 == 0, 0.0, hh)
    tpos = t0 + lax.broadcasted_iota(jnp.int32, (tm, 1), 0)
    pd = x.shape[1] // len(POOL_WINDOWS)
    for g, w in enumerate(POOL_WINDOWS):
        sl = slice(g * pd, (g + 1) * pd)
        hg = h[:, sl]
        cur = jnp.concatenate([hh[:, sl], hg], axis=0)
        sh = 1
        while sh < w:
            cur = cur + pltpu.roll(cur, sh, 0)
            sh *= 2
        wsum = cur[HALO:, :]
        inv_cnt = 1.0 / jnp.minimum(tpos + 1, w).astype(F32)
        pooled = (wsum * inv_cnt - hg).astype(BF16)
        yg = jnp.dot(pooled, pw_ref[g], preferred_element_type=F32)
        x3_ref[:, sl] = x[:, sl] + yg * ps_ref[:, sl]
    x3 = x3_ref[...]
    h3 = _rms(x3, g2_ref[...])
    h3_ref[...] = h3
    logits = jnp.dot(h3, rw_ref[...], preferred_element_type=F32, precision=lax.Precision.HIGHEST)
    lane = lax.broadcasted_iota(jnp.int32, logits.shape, 1)
    logits = jnp.where(lane < N_EXPERTS, logits, -jnp.inf)
    m1 = jnp.max(logits, axis=-1, keepdims=True)
    i1 = jnp.min(jnp.where(logits == m1, lane, LANES), axis=-1, keepdims=True)
    is1 = lane == i1
    rest = jnp.where(is1, -jnp.inf, logits)
    m2 = jnp.max(rest, axis=-1, keepdims=True)
    i2 = jnp.min(jnp.where(rest == m2, lane, LANES), axis=-1, keepdims=True)
    is2 = lane == i2
    e2 = jnp.exp(m2 - m1)
    p1 = 1.0 / (1.0 + e2)
    p2 = e2 / (1.0 + e2)
    gates_ref[...] = jnp.where(is1, p1, jnp.where(is2, p2, 0.0))
    sel_ref[...] = jnp.where(is1 | is2, 1.0, 0.0)


def _pool_router(x, g1, pool_w, pool_scale, g2, router_w, *, seq, tm=256):
    T, D = x.shape
    hb = tm // HALO
    full2 = lambda a: pl.BlockSpec(a.shape, lambda i: (0, 0))
    row_spec = lambda n: pl.BlockSpec((tm, n), lambda i: (i, 0))
    return pl.pallas_call(
        functools.partial(_pool_router_kernel, tm=tm, seq=seq),
        grid=(T // tm,),
        in_specs=[
            row_spec(D),
            pl.BlockSpec((HALO, D), lambda i: (jnp.maximum(i * hb - 1, 0), 0)),
            full2(g1),
            pl.BlockSpec(pool_w.shape, lambda i: (0, 0, 0)),
            full2(pool_scale), full2(g2), full2(router_w),
        ],
        out_specs=[row_spec(D), row_spec(D), row_spec(LANES), row_spec(LANES)],
        out_shape=[
            jax.ShapeDtypeStruct((T, D), F32),
            jax.ShapeDtypeStruct((T, D), F32),
            jax.ShapeDtypeStruct((T, LANES), F32),
            jax.ShapeDtypeStruct((T, LANES), F32),
        ],
        compiler_params=_cparams(("parallel",)),
        name="pool_router",
    )(x, x, g1, pool_w, pool_scale, g2, router_w)


def _rank_kernel(sel_ref, gates_ref, pos_ref, pw_ref, tile_ref, meta_ref, rank_ref, *, tb, tme):
    T = sel_ref.shape[0]
    nb = T // tb
    r = lax.broadcasted_iota(jnp.int32, (tb, tb), 0)
    c = lax.broadcasted_iota(jnp.int32, (tb, tb), 1)
    tri = jnp.where(c < r, 1.0, 0.0).astype(BF16)

    def count(b, carry):
        rows = pl.ds(pl.multiple_of(b * tb, tb), tb)
        blk = sel_ref[rows, :]
        rank_ref[rows, :] = jnp.dot(tri, blk.astype(BF16), preferred_element_type=F32) + carry
        return carry + jnp.sum(blk, axis=0, keepdims=True)

    counts = lax.fori_loop(0, nb, count, jnp.zeros((1, LANES), F32))
    padded = jnp.floor((counts + (tme - 1)) * (1.0 / tme)) * tme
    lane8 = lax.broadcasted_iota(jnp.int32, (8, LANES), 1)
    inc = jnp.broadcast_to(padded, (8, LANES))
    for sh in (1, 2, 4):
        inc = inc + jnp.where(lane8 >= sh, pltpu.roll(inc, sh, 1), 0.0)
    offs = inc[0:1, :] - padded

    def place(b, carry):
        rows = pl.ds(pl.multiple_of(b * tb, tb), tb)
        sel = sel_ref[rows, :] > 0.0
        gates = gates_ref[rows, :]
        slot = offs + rank_ref[rows, :]
        lane = lax.broadcasted_iota(jnp.int32, (tb, LANES), 1)
        first = jnp.min(jnp.where(sel, lane, LANES), axis=-1, keepdims=True)
        lo = lane == first
        hi = sel & jnp.logical_not(lo)
        pick = lambda m, a: jnp.sum(jnp.where(m, a, 0.0), axis=-1, keepdims=True)
        pos_ref[rows, :] = jnp.where(lane == 0, pick(lo, slot),
                                     jnp.where(lane == 1, pick(hi, slot), 0.0)).astype(jnp.int32)
        pw_ref[rows, :] = jnp.where(lane == 0, pick(lo, gates), jnp.where(lane == 1, pick(hi, gates), 0.0))
        return carry

    lax.fori_loop(0, nb, place, 0)

    first_slot = (lax.broadcasted_iota(jnp.int32, (LANES, LANES), 0) * tme).astype(F32)
    lane_t = lax.broadcasted_iota(jnp.int32, (LANES, LANES), 1)
    ended = jnp.where((lane_t < N_EXPERTS) & (first_slot >= inc[0:1, :]), 1.0, 0.0)
    expert = jnp.minimum(jnp.sum(ended, axis=-1, keepdims=True), N_EXPERTS - 1.0)
    total = jnp.max(inc[0:1, :], axis=-1, keepdims=True)
    tile_ref[...] = jnp.where(lane_t == 0, expert, jnp.where(lane_t == 1, total * (1.0 / tme), 0.0)
                              ).astype(jnp.int32)
    row8 = lax.broadcasted_iota(jnp.int32, (8, LANES), 0)
    meta_ref[...] = jnp.where(row8 == 0, counts, jnp.where(row8 == 1, offs, jnp.where(row8 == 2, padded, 0.0))
                              ).astype(jnp.int32)


def _rank_slots(sel, gates, *, tme, tb=512):
    T = sel.shape[0]
    vm = pl.BlockSpec(memory_space=pltpu.VMEM)
    return pl.pallas_call(
        functools.partial(_rank_kernel, tb=tb, tme=tme),
        in_specs=[vm, vm],
        out_specs=[vm, vm, vm, vm],
        out_shape=[
            jax.ShapeDtypeStruct((T, LANES), jnp.int32),
            jax.ShapeDtypeStruct((T, LANES), F32),
            jax.ShapeDtypeStruct((LANES, LANES), jnp.int32),
            jax.ShapeDtypeStruct((8, LANES), jnp.int32),
        ],
        scratch_shapes=[pltpu.VMEM((T, LANES), F32)],
        compiler_params=pltpu.CompilerParams(vmem_limit_bytes=VMEM_LIMIT),
        name="rank_slots",
    )(sel, gates)


def _row_copy(src, s, dst, d, sem):
    return pltpu.make_async_copy(src.at[pl.ds(s, 1)], dst.at[pl.ds(d, 1)], sem)


def _dispatch_kernel(cnt_ref, off_ref, pad_ref, nt_ref, pos_ref, h_hbm, o_hbm, zero_ref, sem, zsem, *, td, tme):
    i = pl.program_id(0)
    zrows = zero_ref.shape[0]

    def start(r, c):
        for k in range(2):
            _row_copy(h_hbm, i * td + r, o_hbm, pos_ref[0, 0, 2 * r + k], sem).start()
        return c

    lax.fori_loop(0, td, start, 0)

    @pl.when(i == 0)
    def _():
        zero_ref[...] = jnp.zeros(zero_ref.shape, zero_ref.dtype)
        for e in range(N_EXPERTS):
            lo = off_ref[e] + cnt_ref[e]
            hi = off_ref[e] + pad_ref[e]

            def zstart(s, c):
                _row_copy(zero_ref, 0, o_hbm, s, zsem).start()
                return c

            def zwait(s, c):
                _row_copy(zero_ref, 0, o_hbm, s, zsem).wait()
                return c

            lax.fori_loop(lo, hi, zstart, 0)
            lax.fori_loop(lo, hi, zwait, 0)

        def chunk(j):
            return pltpu.make_async_copy(zero_ref, o_hbm.at[pl.ds(pl.multiple_of(j * zrows, zrows), zrows)], zsem)

        c_lo = nt_ref[0] * (tme // zrows)
        c_hi = o_hbm.shape[0] // zrows
        lax.fori_loop(c_lo, c_hi, lambda j, c: (chunk(j).start(), c)[1], 0)
        lax.fori_loop(c_lo, c_hi, lambda j, c: (chunk(j).wait(), c)[1], 0)

    def wait(r, c):
        for k in range(2):
            _row_copy(h_hbm, i * td + r, o_hbm, pos_ref[0, 0, 2 * r + k], sem).wait()
        return c

    lax.fori_loop(0, td, wait, 0)


def _dispatch(h, pos2, counts, offs, padded, n_tiles, *, n_slots, tme, td=256, zrows=256):
    T, D = h.shape
    nblk = T // td
    pos_blk = pos2.reshape(nblk, 1, 2 * td)
    grid_spec = pltpu.PrefetchScalarGridSpec(
        num_scalar_prefetch=4,
        grid=(nblk,),
        in_specs=[
            pl.BlockSpec((1, 1, 2 * td), lambda i, *_: (i, 0, 0), memory_space=pltpu.SMEM),
            pl.BlockSpec(memory_space=pl.ANY),
        ],
        out_specs=pl.BlockSpec(memory_space=pl.ANY),
        scratch_shapes=[
            pltpu.VMEM((zrows, D), h.dtype),
            pltpu.SemaphoreType.DMA(()),
            pltpu.SemaphoreType.DMA(()),
        ],
    )
    return pl.pallas_call(
        functools.partial(_dispatch_kernel, td=td, tme=tme),
        grid_spec=grid_spec,
        out_shape=jax.ShapeDtypeStruct((n_slots, D), h.dtype),
        compiler_params=pltpu.CompilerParams(dimension_semantics=("arbitrary",), vmem_limit_bytes=VMEM_LIMIT),
        name="dispatch",
    )(counts, offs, padded, n_tiles, pos_blk, h)


def _experts_kernel(te_ref, nt_ref, x_ref, wg_ref, wu_ref, wd_ref, o_ref, xb_ref):
    i = pl.program_id(0)
    k = pl.program_id(1)

    @pl.when(i < nt_ref[0])
    def _():
        @pl.when(k == 0)
        def _():
            xb_ref[...] = x_ref[...].astype(xb_ref.dtype)
            o_ref[...] = jnp.zeros(o_ref.shape, o_ref.dtype)

        xb = xb_ref[...]
        g = jnp.dot(xb, wg_ref[0], preferred_element_type=F32)
        u = jnp.dot(xb, wu_ref[0], preferred_element_type=F32)
        a = (g * jax.nn.sigmoid(g) * u).astype(xb.dtype)
        o_ref[...] += jnp.dot(a, wd_ref[0], preferred_element_type=F32)

    @pl.when((i >= nt_ref[0]) & (k == 0))
    def _():
        o_ref[...] = jnp.zeros(o_ref.shape, o_ref.dtype)


def _experts(xs, tile_expert, n_tiles, wg, wu, wd, *, tme, tf=256):
    NS, D = xs.shape
    F = wg.shape[2]
    nt = NS // tme

    def row_map(i, k, te, n):
        return (jnp.minimum(i, n[0] - 1), 0)

    grid_spec = pltpu.PrefetchScalarGridSpec(
        num_scalar_prefetch=2,
        grid=(nt, F // tf),
        in_specs=[
            pl.BlockSpec((tme, D), row_map),
            pl.BlockSpec((1, D, tf), lambda i, k, te, n: (te[i], 0, jnp.where(i < n[0], k, F // tf - 1))),
            pl.BlockSpec((1, D, tf), lambda i, k, te, n: (te[i], 0, jnp.where(i < n[0], k, F // tf - 1))),
            pl.BlockSpec((1, tf, D), lambda i, k, te, n: (te[i], jnp.where(i < n[0], k, F // tf - 1), 0)),
        ],
        out_specs=pl.BlockSpec((tme, D), lambda i, k, te, n: (i, 0)),
        scratch_shapes=[pltpu.VMEM((tme, D), BF16)],
    )
    return pl.pallas_call(
        _experts_kernel,
        grid_spec=grid_spec,
        out_shape=jax.ShapeDtypeStruct((NS, D), F32),
        compiler_params=_cparams(("arbitrary", "arbitrary")),
        name="experts",
    )(tile_expert, n_tiles, xs, wg, wu, wd)


def _combine_kernel(pos_ref, x_ref, pw_ref, g_ref, y_hbm, o_ref, lo_ref, hi_ref, sem, *, tc):
    def start(r, c):
        _row_copy(y_hbm, pos_ref[0, 0, 2 * r], lo_ref, r, sem).start()
        _row_copy(y_hbm, pos_ref[0, 0, 2 * r + 1], hi_ref, r, sem).start()
        return c

    def wait(r, c):
        _row_copy(y_hbm, pos_ref[0, 0, 2 * r], lo_ref, r, sem).wait()
        _row_copy(y_hbm, pos_ref[0, 0, 2 * r + 1], hi_ref, r, sem).wait()
        return c

    lax.fori_loop(0, tc, start, 0)
    lax.fori_loop(0, tc, wait, 0)
    pw = pw_ref[...]
    y = x_ref[...] + pw[:, 0:1] * lo_ref[...] + pw[:, 1:2] * hi_ref[...]
    o_ref[...] = _rms(y, g_ref[...])


def _combine(x, pw, g, ys, pos2, *, tc=256):
    T, D = x.shape
    nblk = T // tc
    pos_blk = pos2.reshape(nblk, 1, 2 * tc)
    return pl.pallas_call(
        functools.partial(_combine_kernel, tc=tc),
        grid=(nblk,),
        in_specs=[
            pl.BlockSpec((1, 1, 2 * tc), lambda i: (i, 0, 0), memory_space=pltpu.SMEM),
            pl.BlockSpec((tc, D), lambda i: (i, 0)),
            pl.BlockSpec((tc, LANES), lambda i: (i, 0)),
            pl.BlockSpec((1, D), lambda i: (0, 0)),
            pl.BlockSpec(memory_space=pl.ANY),
        ],
        out_specs=pl.BlockSpec((tc, D), lambda i: (i, 0)),
        out_shape=jax.ShapeDtypeStruct((T, D), F32),
        scratch_shapes=[
            pltpu.VMEM((tc, D), F32),
            pltpu.VMEM((tc, D), F32),
            pltpu.SemaphoreType.DMA(()),
        ],
        compiler_params=_cparams(("arbitrary",)),
        name="combine",
    )(pos_blk, x, pw, g, ys)


def _rope_tables(seq):
    pos = jnp.arange(seq, dtype=F32)
    inv_freq = ROPE_THETA ** (-jnp.arange(0, QK_ROPE, 2, dtype=F32) / QK_ROPE)
    ang = pos[:, None] * inv_freq[None, :]
    cos, sin = jnp.cos(ang), jnp.sin(ang)
    z32 = jnp.zeros_like(cos)
    z64 = jnp.zeros((seq, LANES - QK_ROPE), F32)
    cos_f = jnp.concatenate([cos, cos, z64], axis=1)
    sin_a = jnp.concatenate([-sin, z32, z64], axis=1)
    sin_b = jnp.concatenate([z32, sin, z64], axis=1)
    return cos_f, sin_a, sin_b


def kernel(x, norm_mix0, w_in, conv_w, q_norm, w_uq, kv_norm, w_ukv, w_out, norm_ffn0, ffn_w_gate,
           ffn_w_up, ffn_w_down, norm_mix1, pool_w, pool_scale, norm_ffn1, router_w, moe_w_gate,
           moe_w_up, moe_w_down, final_norm):
    B, S, D = x.shape
    T = B * S
    H = MLA_HEADS
    tme = 1024
    n_slots = 2 * T + N_EXPERTS * tme

    xf = x.reshape(T, D)

    w_in_p = jnp.pad(w_in[0], ((0, 0), (0, IN_COLS_PAD - w_in.shape[2]))).astype(BF16)
    wq = w_uq[0].reshape(Q_LORA, H, QK_NOPE + QK_ROPE)
    wq = jnp.pad(wq, ((0, 0), (0, 0), (0, HEAD_PAD - QK_NOPE - QK_ROPE))).reshape(Q_LORA, H * HEAD_PAD).astype(BF16)
    wkv = w_ukv[0].reshape(KV_LORA, H, QK_NOPE + V_HEAD)
    wk = wkv[:, :, :QK_NOPE].reshape(KV_LORA, H * QK_NOPE).astype(BF16)
    wv = wkv[:, :, QK_NOPE:].reshape(KV_LORA, H * V_HEAD).astype(BF16)
    wo_a = w_out[0, :CONV_DIM].astype(BF16)
    wo_b = w_out[0, CONV_DIM:].astype(BF16)
    rw = jnp.pad(router_w[0], ((0, 0), (0, LANES - N_EXPERTS)))
    cos_f, sin_a, sin_b = _rope_tables(S)

    proj = _norm_inproj(xf, norm_mix0, w_in_p)
    y_a = _gated_conv(proj, conv_w[0], seq=S)
    q, k, v = _qkv_proj(proj, q_norm, kv_norm, wq, wk, wv, cos_f, sin_a, sin_b, seq=S)
    y_b = _flash_attention(q, k, v, batch=B, seq=S)
    x1, h1 = _out_proj(y_a, y_b, wo_a, wo_b, xf, norm_ffn0)
    x2 = _dense_ffn(h1, ffn_w_gate[0].astype(BF16), ffn_w_up[0].astype(BF16), ffn_w_down[0].astype(BF16), x1)

    x3, h3, gates, sel = _pool_router(x2, norm_mix1, pool_w[0].astype(BF16), pool_scale, norm_ffn1, rw, seq=S)
    pos, pw, tiles, meta = _rank_slots(sel, gates, tme=tme)
    pos2 = pos[:, :2]
    nt = n_slots // tme
    n_tiles = tiles[0:1, 1]
    xs = _dispatch(h3, pos2, meta[0, :N_EXPERTS], meta[1, :N_EXPERTS], meta[2, :N_EXPERTS], n_tiles,
                   n_slots=n_slots, tme=tme)
    ys = _experts(xs, tiles[:nt, 0], n_tiles, moe_w_gate[0].astype(BF16), moe_w_up[0].astype(BF16),
                  moe_w_down[0].astype(BF16), tme=tme)
    out = _combine(x3, pw, final_norm.reshape(1, D), ys, pos2)
    return out.reshape(B, S, D)
```

```python
import functools

import jax
import jax.numpy as jnp
from jax import lax
from jax.experimental import pallas as pl
from jax.experimental.pallas import tpu as pltpu

F32 = jnp.float32
BF16 = jnp.bfloat16

EPS = 1e-6
CONV_DIM = 1024
MLA_HEADS = 8
Q_LORA = 512
KV_LORA = 256
QK_NOPE = 128
QK_ROPE = 64
V_HEAD = 128
ROPE_THETA = 10000.0
POOL_WINDOWS = (2, 4, 8, 16)
N_EXPERTS = 8

LANES = 128
HEAD_PAD = 256
IN_COLS_PAD = 4096
LAT_OFF = 3 * CONV_DIM
HALO = 16
VMEM_LIMIT = 56 * 1024 * 1024


def _cparams(sem):
    return pltpu.CompilerParams(dimension_semantics=sem, vmem_limit_bytes=VMEM_LIMIT)


def _rms(x, g):
    return x * lax.rsqrt(jnp.mean(x * x, axis=-1, keepdims=True) + EPS) * g


def _norm_inproj_kernel(x_ref, g_ref, w_ref, o_ref, h_ref):
    @pl.when(pl.program_id(1) == 0)
    def _():
        h_ref[...] = _rms(x_ref[...], g_ref[...]).astype(BF16)

    o_ref[...] = jnp.dot(h_ref[...], w_ref[...], preferred_element_type=F32).astype(o_ref.dtype)


def _norm_inproj(x, g, w, *, tm=1024, tn=1024):
    T, D = x.shape
    N = w.shape[1]
    return pl.pallas_call(
        _norm_inproj_kernel,
        grid=(T // tm, N // tn),
        in_specs=[
            pl.BlockSpec((tm, D), lambda i, j: (i, 0)),
            pl.BlockSpec((1, D), lambda i, j: (0, 0)),
            pl.BlockSpec((D, tn), lambda i, j: (0, j)),
        ],
        out_specs=pl.BlockSpec((tm, tn), lambda i, j: (i, j)),
        out_shape=jax.ShapeDtypeStruct((T, N), BF16),
        scratch_shapes=[pltpu.VMEM((tm, D), BF16)],
        compiler_params=_cparams(("parallel", "arbitrary")),
        name="norm_inproj",
    )(x, g, w)


def _conv_kernel(xin_ref, gb_ref, gc_ref, hx_ref, hc_ref, w_ref, o_ref, *, ts, seq):
    i = pl.program_id(0)
    u = gc_ref[...].astype(F32) * xin_ref[...].astype(F32)
    hu = hc_ref[...].astype(F32) * hx_ref[...].astype(F32)
    hu = jnp.where((i * ts) % seq == 0, 0.0, hu)
    hm1 = hu[HALO - 1:HALO, :]
    hm2 = hu[HALO - 2:HALO - 1, :]
    row = lax.broadcasted_iota(jnp.int32, u.shape, 0)
    u1 = jnp.where(row == 0, hm1, pltpu.roll(u, 1, 0))
    u2 = jnp.where(row == 0, hm2, jnp.where(row == 1, hm1, pltpu.roll(u, 2, 0)))
    w = w_ref[...]
    y = w[0:1, :] * u2 + w[1:2, :] * u1 + w[2:3, :] * u
    o_ref[...] = (gb_ref[...].astype(F32) * y).astype(o_ref.dtype)


def _gated_conv(proj, conv_w, *, seq, ts=512):
    T = proj.shape[0]
    C = CONV_DIM
    hb = ts // HALO
    halo_map = lambda c: (lambda i: (jnp.maximum(i * hb - 1, 0), c))
    return pl.pallas_call(
        functools.partial(_conv_kernel, ts=ts, seq=seq),
        grid=(T // ts,),
        in_specs=[
            pl.BlockSpec((ts, C), lambda i: (i, 0)),
            pl.BlockSpec((ts, C), lambda i: (i, 1)),
            pl.BlockSpec((ts, C), lambda i: (i, 2)),
            pl.BlockSpec((HALO, C), halo_map(0)),
            pl.BlockSpec((HALO, C), halo_map(2)),
            pl.BlockSpec(conv_w.shape, lambda i: (0, 0)),
        ],
        out_specs=pl.BlockSpec((ts, C), lambda i: (i, 0)),
        out_shape=jax.ShapeDtypeStruct((T, C), BF16),
        compiler_params=_cparams(("parallel",)),
        name="gated_conv",
    )(proj, proj, proj, proj, proj, conv_w)


def _rope128(blk, cos_f, sin_a, sin_b):
    return blk * cos_f + pltpu.roll(blk, 96, 1) * sin_a + pltpu.roll(blk, 32, 1) * sin_b


def _qkv_kernel(lat_ref, qn_ref, kvn_ref, wq_ref, wk_ref, wv_ref, cos_ref, sa_ref, sb_ref,
                q_ref, k_ref, v_ref, *, scale):
    lat = lat_ref[...].astype(F32)
    cq = _rms(lat[:, :Q_LORA], qn_ref[...]).astype(BF16)
    ckv = _rms(lat[:, Q_LORA:Q_LORA + KV_LORA], kvn_ref[...]).astype(BF16)
    kr = lat[:, Q_LORA + KV_LORA:Q_LORA + KV_LORA + LANES]
    cos_f, sin_a, sin_b = cos_ref[...], sa_ref[...], sb_ref[...]
    q = jnp.dot(cq, wq_ref[...], preferred_element_type=F32) * scale
    kn = jnp.dot(ckv, wk_ref[...], preferred_element_type=F32)
    v_ref[...] = jnp.dot(ckv, wv_ref[...], preferred_element_type=F32).astype(v_ref.dtype)
    kr_rot = _rope128(kr, cos_f, sin_a, sin_b).astype(k_ref.dtype)
    for h in range(MLA_HEADS):
        o = h * HEAD_PAD
        q_ref[:, o:o + QK_NOPE] = q[:, o:o + QK_NOPE].astype(q_ref.dtype)
        q_ref[:, o + QK_NOPE:o + HEAD_PAD] = _rope128(
            q[:, o + QK_NOPE:o + HEAD_PAD], cos_f, sin_a, sin_b).astype(q_ref.dtype)
        k_ref[:, o:o + QK_NOPE] = kn[:, h * QK_NOPE:(h + 1) * QK_NOPE].astype(k_ref.dtype)
        k_ref[:, o + QK_NOPE:o + HEAD_PAD] = kr_rot


def _qkv_proj(proj, q_norm, kv_norm, wq, wk, wv, cos_f, sin_a, sin_b, *, seq, tm=512):
    T = proj.shape[0]
    H = MLA_HEADS
    lat_blk = LAT_OFF // 1024
    nseq = seq // tm
    tab = pl.BlockSpec((tm, LANES), lambda i: (i % nseq, 0))
    full = lambda a: pl.BlockSpec(a.shape, lambda i: (0, 0))
    return pl.pallas_call(
        functools.partial(_qkv_kernel, scale=float((QK_NOPE + QK_ROPE) ** -0.5)),
        grid=(T // tm,),
        in_specs=[
            pl.BlockSpec((tm, 1024), lambda i: (i, lat_blk)),
            full(q_norm), full(kv_norm), full(wq), full(wk), full(wv),
            tab, tab, tab,
        ],
        out_specs=[
            pl.BlockSpec((tm, H * HEAD_PAD), lambda i: (i, 0)),
            pl.BlockSpec((tm, H * HEAD_PAD), lambda i: (i, 0)),
            pl.BlockSpec((tm, H * V_HEAD), lambda i: (i, 0)),
        ],
        out_shape=[
            jax.ShapeDtypeStruct((T, H * HEAD_PAD), BF16),
            jax.ShapeDtypeStruct((T, H * HEAD_PAD), BF16),
            jax.ShapeDtypeStruct((T, H * V_HEAD), BF16),
        ],
        compiler_params=_cparams(("parallel",)),
        name="qkv_proj",
    )(proj, q_norm, kv_norm, wq, wk, wv, cos_f, sin_a, sin_b)


def _flash_kernel(q_ref, k_ref, v_ref, o_ref, m_ref, l_ref, acc_ref, *, tq):
    qi = pl.program_id(2)
    q = q_ref[...]
    m_ref[...] = jnp.full(m_ref.shape, -jnp.inf, F32)
    l_ref[...] = jnp.zeros(l_ref.shape, F32)
    acc_ref[...] = jnp.zeros(acc_ref.shape, F32)

    def block(j, masked):
        start = pl.multiple_of(j * tq, tq)
        k = k_ref[pl.ds(start, tq), :]
        v = v_ref[pl.ds(start, tq), :]
        s = lax.dot_general(q, k, (((1,), (1,)), ((), ())), preferred_element_type=F32)
        if masked:
            row = lax.broadcasted_iota(jnp.int32, s.shape, 0)
            col = lax.broadcasted_iota(jnp.int32, s.shape, 1)
            s = jnp.where(col <= row, s, -jnp.inf)
        m_prev = m_ref[...]
        m_new = jnp.maximum(m_prev, jnp.max(s, axis=-1, keepdims=True))
        alpha = jnp.exp(m_prev - m_new)
        p = jnp.exp(s - m_new)
        l_ref[...] = alpha * l_ref[...] + jnp.sum(p, axis=-1, keepdims=True)
        acc_ref[...] = alpha * acc_ref[...] + jnp.dot(p.astype(v.dtype), v, preferred_element_type=F32)
        m_ref[...] = m_new

    def body(j, c):
        block(j, False)
        return c

    lax.fori_loop(0, qi, body, 0)
    block(qi, True)
    o_ref[...] = (acc_ref[...] / l_ref[...]).astype(o_ref.dtype)


def _flash_attention(q, k, v, *, batch, seq, tq=512):
    T = q.shape[0]
    H = MLA_HEADS
    nq = seq // tq
    return pl.pallas_call(
        functools.partial(_flash_kernel, tq=tq),
        grid=(batch, H, nq),
        in_specs=[
            pl.BlockSpec((tq, HEAD_PAD), lambda b, h, i: (b * nq + i, h)),
            pl.BlockSpec((seq, HEAD_PAD), lambda b, h, i: (b, h)),
            pl.BlockSpec((seq, V_HEAD), lambda b, h, i: (b, h)),
        ],
        out_specs=pl.BlockSpec((tq, V_HEAD), lambda b, h, i: (b * nq + i, h)),
        out_shape=jax.ShapeDtypeStruct((T, H * V_HEAD), BF16),
        scratch_shapes=[
            pltpu.VMEM((tq, 1), F32),
            pltpu.VMEM((tq, 1), F32),
            pltpu.VMEM((tq, V_HEAD), F32),
        ],
        compiler_params=_cparams(("parallel", "parallel", "arbitrary")),
        name="flash_attention",
    )(q, k, v)


def _outproj_kernel(ya_ref, yb_ref, wa_ref, wb_ref, x_ref, g_ref, x1_ref, h1_ref):
    acc = jnp.dot(ya_ref[...], wa_ref[...], preferred_element_type=F32)
    acc = acc + jnp.dot(yb_ref[...], wb_ref[...], preferred_element_type=F32)
    x1 = x_ref[...] + acc
    x1_ref[...] = x1
    h1_ref[...] = _rms(x1, g_ref[...]).astype(h1_ref.dtype)


def _out_proj(ya, yb, wa, wb, x, g, *, tm=512):
    T, D = x.shape
    full = lambda a: pl.BlockSpec(a.shape, lambda i: (0, 0))
    return pl.pallas_call(
        _outproj_kernel,
        grid=(T // tm,),
        in_specs=[
            pl.BlockSpec((tm, ya.shape[1]), lambda i: (i, 0)),
            pl.BlockSpec((tm, yb.shape[1]), lambda i: (i, 0)),
            full(wa), full(wb),
            pl.BlockSpec((tm, D), lambda i: (i, 0)),
            full(g),
        ],
        out_specs=[pl.BlockSpec((tm, D), lambda i: (i, 0)), pl.BlockSpec((tm, D), lambda i: (i, 0))],
        out_shape=[jax.ShapeDtypeStruct((T, D), F32), jax.ShapeDtypeStruct((T, D), BF16)],
        compiler_params=_cparams(("parallel",)),
        name="out_proj",
    )(ya, yb, wa, wb, x, g)


def _ffn_kernel(h_ref, wg_ref, wu_ref, wd_ref, x_ref, o_ref):
    k = pl.program_id(1)

    @pl.when(k == 0)
    def _():
        o_ref[...] = x_ref[...]

    h = h_ref[...]
    g = jnp.dot(h, wg_ref[...], preferred_element_type=F32)
    u = jnp.dot(h, wu_ref[...], preferred_element_type=F32)
    a = (g * jax.nn.sigmoid(g) * u).astype(h.dtype)
    o_ref[...] += jnp.dot(a, wd_ref[...], preferred_element_type=F32)


def _dense_ffn(h, wg, wu, wd, x, *, tm=512, tf=512):
    T, D = x.shape
    F = wg.shape[1]
    return pl.pallas_call(
        _ffn_kernel,
        grid=(T // tm, F // tf),
        in_specs=[
            pl.BlockSpec((tm, D), lambda i, k: (i, 0)),
            pl.BlockSpec((D, tf), lambda i, k: (0, k)),
            pl.BlockSpec((D, tf), lambda i, k: (0, k)),
            pl.BlockSpec((tf, D), lambda i, k: (k, 0)),
            pl.BlockSpec((tm, D), lambda i, k: (i, 0)),
        ],
        out_specs=pl.BlockSpec((tm, D), lambda i, k: (i, 0)),
        out_shape=jax.ShapeDtypeStruct((T, D), F32),
        compiler_params=_cparams(("parallel", "arbitrary")),
        name="dense_ffn",
    )(h, wg, wu, wd, x)


def _pool_router_kernel(x_ref, halo_ref, g1_ref, pw_ref, ps_ref, g2_ref, rw_ref,
                        x3_ref, h3_ref, gates_ref, sel_ref, *, tm, seq):
    i = pl.program_id(0)
    x = x_ref[...]
    g1 = g1_ref[...]
    h = _rms(x, g1)
    hh = _rms(halo_ref[...], g1)
    t0 = (i * tm) % seq
    hh = jnp.where(t0 == 0, 0.0, hh)
    tpos = t0 + lax.broadcasted_iota(jnp.int32, (tm, 1), 0)
    pd = x.shape[1] // len(POOL_WINDOWS)
    for g, w in enumerate(POOL_WINDOWS):
        sl = slice(g * pd, (g + 1) * pd)
        hg = h[:, sl]
        cur = jnp.concatenate([hh[:, sl], hg], axis=0)
        sh = 1
        while sh < w:
            cur = cur + pltpu.roll(cur, sh, 0)
            sh *= 2
        wsum = cur[HALO:, :]
        inv_cnt = 1.0 / jnp.minimum(tpos + 1, w).astype(F32)
        pooled = (wsum * inv_cnt - hg).astype(BF16)
        yg = jnp.dot(pooled, pw_ref[g], preferred_element_type=F32)
        x3_ref[:, sl] = x[:, sl] + yg * ps_ref[:, sl]
    x3 = x3_ref[...]
    h3 = _rms(x3, g2_ref[...])
    h3_ref[...] = h3
    logits = jnp.dot(h3, rw_ref[...], preferred_element_type=F32, precision=lax.Precision.HIGHEST)
    lane = lax.broadcasted_iota(jnp.int32, logits.shape, 1)
    logits = jnp.where(lane < N_EXPERTS, logits, -jnp.inf)
    m1 = jnp.max(logits, axis=-1, keepdims=True)
    i1 = jnp.min(jnp.where(logits == m1, lane, LANES), axis=-1, keepdims=True)
    is1 = lane == i1
    rest = jnp.where(is1, -jnp.inf, logits)
    m2 = jnp.max(rest, axis=-1, keepdims=True)
    i2 = jnp.min(jnp.where(rest == m2, lane, LANES), axis=-1, keepdims=True)
    is2 = lane == i2
    e2 = jnp.exp(m2 - m1)
    p1 = 1.0 / (1.0 + e2)
    p2 = e2 / (1.0 + e2)
    gates_ref[...] = jnp.where(is1, p1, jnp.where(is2, p2, 0.0))
    sel_ref[...] = jnp.where(is1 | is2, 1.0, 0.0)


def _pool_router(x, g1, pool_w, pool_scale, g2, router_w, *, seq, tm=256):
    T, D = x.shape
    hb = tm // HALO
    full2 = lambda a: pl.BlockSpec(a.shape, lambda i: (0, 0))
    row_spec = lambda n: pl.BlockSpec((tm, n), lambda i: (i, 0))
    return pl.pallas_call(
        functools.partial(_pool_router_kernel, tm=tm, seq=seq),
        grid=(T // tm,),
        in_specs=[
            row_spec(D),
            pl.BlockSpec((HALO, D), lambda i: (jnp.maximum(i * hb - 1, 0), 0)),
            full2(g1),
            pl.BlockSpec(pool_w.shape, lambda i: (0, 0, 0)),
            full2(pool_scale), full2(g2), full2(router_w),
        ],
        out_specs=[row_spec(D), row_spec(D), row_spec(LANES), row_spec(LANES)],
        out_shape=[
            jax.ShapeDtypeStruct((T, D), F32),
            jax.ShapeDtypeStruct((T, D), F32),
            jax.ShapeDtypeStruct((T, LANES), F32),
            jax.ShapeDtypeStruct((T, LANES), F32),
        ],
        compiler_params=_cparams(("parallel",)),
        name="pool_router",
    )(x, x, g1, pool_w, pool_scale, g2, router_w)


def _rank_kernel(sel_ref, gates_ref, pos_ref, pw_ref, tile_ref, meta_ref, rank_ref, *, tb, tme):
    T = sel_ref.shape[0]
    nb = T // tb
    r = lax.broadcasted_iota(jnp.int32, (tb, tb), 0)
    c = lax.broadcasted_iota(jnp.int32, (tb, tb), 1)
    tri = jnp.where(c < r, 1.0, 0.0).astype(BF16)

    def count(b, carry):
        rows = pl.ds(pl.multiple_of(b * tb, tb), tb)
        blk = sel_ref[rows, :]
        rank_ref[rows, :] = jnp.dot(tri, blk.astype(BF16), preferred_element_type=F32) + carry
        return carry + jnp.sum(blk, axis=0, keepdims=True)

    counts = lax.fori_loop(0, nb, count, jnp.zeros((1, LANES), F32))
    padded = jnp.floor((counts + (tme - 1)) * (1.0 / tme)) * tme
    lane8 = lax.broadcasted_iota(jnp.int32, (8, LANES), 1)
    inc = jnp.broadcast_to(padded, (8, LANES))
    for sh in (1, 2, 4):
        inc = inc + jnp.where(lane8 >= sh, pltpu.roll(inc, sh, 1), 0.0)
    offs = inc[0:1, :] - padded

    def place(b, carry):
        rows = pl.ds(pl.multiple_of(b * tb, tb), tb)
        sel = sel_ref[rows, :] > 0.0
        gates = gates_ref[rows, :]
        slot = offs + rank_ref[rows, :]
        lane = lax.broadcasted_iota(jnp.int32, (tb, LANES), 1)
        first = jnp.min(jnp.where(sel, lane, LANES), axis=-1, keepdims=True)
        lo = lane == first
        hi = sel & jnp.logical_not(lo)
        pick = lambda m, a: jnp.sum(jnp.where(m, a, 0.0), axis=-1, keepdims=True)
        pos_ref[rows, :] = jnp.where(lane == 0, pick(lo, slot),
                                     jnp.where(lane == 1, pick(hi, slot), 0.0)).astype(jnp.int32)
        pw_ref[rows, :] = jnp.where(lane == 0, pick(lo, gates), jnp.where(lane == 1, pick(hi, gates), 0.0))
        return carry

    lax.fori_loop(0, nb, place, 0)

    first_slot = (lax.broadcasted_iota(jnp.int32, (LANES, LANES), 0) * tme).astype(F32)
    lane_t = lax.broadcasted_iota(jnp.int32, (LANES, LANES), 1)
    ended = jnp.where((lane_t < N_EXPERTS) & (first_slot >= inc[0:1, :]), 1.0, 0.0)
    expert = jnp.minimum(jnp.sum(ended, axis=-1, keepdims=True), N_EXPERTS - 1.0)
    total = jnp.max(inc[0:1, :], axis=-1, keepdims=True)
    tile_ref[...] = jnp.where(lane_t == 0, expert, jnp.where(lane_t == 1, total * (1.0 / tme), 0.0)
                              ).astype(jnp.int32)
    row8 = lax.broadcasted_iota(jnp.int32, (8, LANES), 0)
    meta_ref[...] = jnp.where(row8 == 0, counts, jnp.where(row8 == 1, offs, jnp.where(row8 == 2, padded, 0.0))
                              ).astype(jnp.int32)


def _rank_slots(sel, gates, *, tme, tb=512):
    T = sel.shape[0]
    vm = pl.BlockSpec(memory_space=pltpu.VMEM)
    return pl.pallas_call(
        functools.partial(_rank_kernel, tb=tb, tme=tme),
        in_specs=[vm, vm],
        out_specs=[vm, vm, vm, vm],
        out_shape=[
            jax.ShapeDtypeStruct((T, LANES), jnp.int32),
            jax.ShapeDtypeStruct((T, LANES), F32),
            jax.ShapeDtypeStruct((LANES, LANES), jnp.int32),
            jax.ShapeDtypeStruct((8, LANES), jnp.int32),
        ],
        scratch_shapes=[pltpu.VMEM((T, LANES), F32)],
        compiler_params=pltpu.CompilerParams(vmem_limit_bytes=VMEM_LIMIT),
        name="rank_slots",
    )(sel, gates)


def _row_copy(src, s, dst, d, sem):
    return pltpu.make_async_copy(src.at[pl.ds(s, 1)], dst.at[pl.ds(d, 1)], sem)


def _dispatch_kernel(cnt_ref, off_ref, pad_ref, nt_ref, pos_ref, h_ref, o_hbm, zero_ref, sem, zsem, *, td, tme):
    i = pl.program_id(0)
    zrows = zero_ref.shape[0]

    def start(r, c):
        for k in range(2):
            _row_copy(h_ref, r, o_hbm, pos_ref[0, 0, 2 * r + k], sem).start()
        return c

    lax.fori_loop(0, td, start, 0, unroll=8)

    @pl.when(i == 0)
    def _():
        zero_ref[...] = jnp.zeros(zero_ref.shape, zero_ref.dtype)
        for e in range(N_EXPERTS):
            lo = off_ref[e] + cnt_ref[e]
            hi = off_ref[e] + pad_ref[e]

            def zstart(s, c):
                _row_copy(zero_ref, 0, o_hbm, s, zsem).start()
                return c

            def zwait(s, c):
                _row_copy(zero_ref, 0, o_hbm, s, zsem).wait()
                return c

            lax.fori_loop(lo, hi, zstart, 0)
            lax.fori_loop(lo, hi, zwait, 0)

        def chunk(j):
            return pltpu.make_async_copy(zero_ref, o_hbm.at[pl.ds(pl.multiple_of(j * zrows, zrows), zrows)], zsem)

        c_lo = nt_ref[0] * (tme // zrows)
        c_hi = o_hbm.shape[0] // zrows
        lax.fori_loop(c_lo, c_hi, lambda j, c: (chunk(j).start(), c)[1], 0)
        lax.fori_loop(c_lo, c_hi, lambda j, c: (chunk(j).wait(), c)[1], 0)

    def wait(r, c):
        for k in range(2):
            _row_copy(h_ref, r, o_hbm, pos_ref[0, 0, 2 * r + k], sem).wait()
        return c

    lax.fori_loop(0, td, wait, 0, unroll=8)


def _dispatch(h, pos2, counts, offs, padded, n_tiles, *, n_slots, tme, td=256, zrows=256):
    T, D = h.shape
    nblk = T // td
    pos_blk = pos2.reshape(nblk, 1, 2 * td)
    grid_spec = pltpu.PrefetchScalarGridSpec(
        num_scalar_prefetch=4,
        grid=(nblk,),
        in_specs=[
            pl.BlockSpec((1, 1, 2 * td), lambda i, *_: (i, 0, 0), memory_space=pltpu.SMEM),
            pl.BlockSpec((td, D), lambda i, *_: (i, 0)),
        ],
        out_specs=pl.BlockSpec(memory_space=pl.ANY),
        scratch_shapes=[
            pltpu.VMEM((zrows, D), h.dtype),
            pltpu.SemaphoreType.DMA(()),
            pltpu.SemaphoreType.DMA(()),
        ],
    )
    return pl.pallas_call(
        functools.partial(_dispatch_kernel, td=td, tme=tme),
        grid_spec=grid_spec,
        out_shape=jax.ShapeDtypeStruct((n_slots, D), h.dtype),
        compiler_params=pltpu.CompilerParams(dimension_semantics=("arbitrary",), vmem_limit_bytes=VMEM_LIMIT),
        name="dispatch",
    )(counts, offs, padded, n_tiles, pos_blk, h)


def _experts_kernel(te_ref, nt_ref, x_ref, wg_ref, wu_ref, wd_ref, o_ref, xb_ref):
    i = pl.program_id(0)
    k = pl.program_id(1)

    @pl.when(i < nt_ref[0])
    def _():
        @pl.when(k == 0)
        def _():
            xb_ref[...] = x_ref[...].astype(xb_ref.dtype)
            o_ref[...] = jnp.zeros(o_ref.shape, o_ref.dtype)

        xb = xb_ref[...]
        g = jnp.dot(xb, wg_ref[0], preferred_element_type=F32)
        u = jnp.dot(xb, wu_ref[0], preferred_element_type=F32)
        a = (g * jax.nn.sigmoid(g) * u).astype(xb.dtype)
        o_ref[...] += jnp.dot(a, wd_ref[0], preferred_element_type=F32)

    @pl.when((i >= nt_ref[0]) & (k == 0))
    def _():
        o_ref[...] = jnp.zeros(o_ref.shape, o_ref.dtype)


def _experts(xs, tile_expert, n_tiles, wg, wu, wd, *, tme, tf=256):
    NS, D = xs.shape
    F = wg.shape[2]
    nt = NS // tme

    def row_map(i, k, te, n):
        return (jnp.minimum(i, n[0] - 1), 0)

    grid_spec = pltpu.PrefetchScalarGridSpec(
        num_scalar_prefetch=2,
        grid=(nt, F // tf),
        in_specs=[
            pl.BlockSpec((tme, D), row_map),
            pl.BlockSpec((1, D, tf), lambda i, k, te, n: (te[i], 0, jnp.where(i < n[0], k, F // tf - 1))),
            pl.BlockSpec((1, D, tf), lambda i, k, te, n: (te[i], 0, jnp.where(i < n[0], k, F // tf - 1))),
            pl.BlockSpec((1, tf, D), lambda i, k, te, n: (te[i], jnp.where(i < n[0], k, F // tf - 1), 0)),
        ],
        out_specs=pl.BlockSpec((tme, D), lambda i, k, te, n: (i, 0)),
        scratch_shapes=[pltpu.VMEM((tme, D), BF16)],
    )
    return pl.pallas_call(
        _experts_kernel,
        grid_spec=grid_spec,
        out_shape=jax.ShapeDtypeStruct((NS, D), F32),
        compiler_params=_cparams(("arbitrary", "arbitrary")),
        name="experts",
    )(tile_expert, n_tiles, xs, wg, wu, wd)


def _combine_kernel(pos_ref, x_ref, pw_ref, g_ref, y_hbm, o_ref, lo_ref, hi_ref, sem, *, tc):
    def start(r, c):
        _row_copy(y_hbm, pos_ref[0, 0, 2 * r], lo_ref, r, sem).start()
        _row_copy(y_hbm, pos_ref[0, 0, 2 * r + 1], hi_ref, r, sem).start()
        return c

    def wait(r, c):
        _row_copy(y_hbm, pos_ref[0, 0, 2 * r], lo_ref, r, sem).wait()
        _row_copy(y_hbm, pos_ref[0, 0, 2 * r + 1], hi_ref, r, sem).wait()
        return c

    lax.fori_loop(0, tc, start, 0)
    lax.fori_loop(0, tc, wait, 0)
    pw = pw_ref[...]
    y = x_ref[...] + pw[:, 0:1] * lo_ref[...] + pw[:, 1:2] * hi_ref[...]
    o_ref[...] = _rms(y, g_ref[...])


def _combine(x, pw, g, ys, pos2, *, tc=256):
    T, D = x.shape
    nblk = T // tc
    pos_blk = pos2.reshape(nblk, 1, 2 * tc)
    return pl.pallas_call(
        functools.partial(_combine_kernel, tc=tc),
        grid=(nblk,),
        in_specs=[
            pl.BlockSpec((1, 1, 2 * tc), lambda i: (i, 0, 0), memory_space=pltpu.SMEM),
            pl.BlockSpec((tc, D), lambda i: (i, 0)),
            pl.BlockSpec((tc, LANES), lambda i: (i, 0)),
            pl.BlockSpec((1, D), lambda i: (0, 0)),
            pl.BlockSpec(memory_space=pl.ANY),
        ],
        out_specs=pl.BlockSpec((tc, D), lambda i: (i, 0)),
        out_shape=jax.ShapeDtypeStruct((T, D), F32),
        scratch_shapes=[
            pltpu.VMEM((tc, D), F32),
            pltpu.VMEM((tc, D), F32),
            pltpu.SemaphoreType.DMA(()),
        ],
        compiler_params=_cparams(("arbitrary",)),
        name="combine",
    )(pos_blk, x, pw, g, ys)


def _rope_tables(seq):
    pos = jnp.arange(seq, dtype=F32)
    inv_freq = ROPE_THETA ** (-jnp.arange(0, QK_ROPE, 2, dtype=F32) / QK_ROPE)
    ang = pos[:, None] * inv_freq[None, :]
    cos, sin = jnp.cos(ang), jnp.sin(ang)
    z32 = jnp.zeros_like(cos)
    z64 = jnp.zeros((seq, LANES - QK_ROPE), F32)
    cos_f = jnp.concatenate([cos, cos, z64], axis=1)
    sin_a = jnp.concatenate([-sin, z32, z64], axis=1)
    sin_b = jnp.concatenate([z32, sin, z64], axis=1)
    return cos_f, sin_a, sin_b


def kernel(x, norm_mix0, w_in, conv_w, q_norm, w_uq, kv_norm, w_ukv, w_out, norm_ffn0, ffn_w_gate,
           ffn_w_up, ffn_w_down, norm_mix1, pool_w, pool_scale, norm_ffn1, router_w, moe_w_gate,
           moe_w_up, moe_w_down, final_norm):
    B, S, D = x.shape
    T = B * S
    H = MLA_HEADS
    tme = 1024
    n_slots = 2 * T + N_EXPERTS * tme

    xf = x.reshape(T, D)

    w_in_p = jnp.pad(w_in[0], ((0, 0), (0, IN_COLS_PAD - w_in.shape[2]))).astype(BF16)
    wq = w_uq[0].reshape(Q_LORA, H, QK_NOPE + QK_ROPE)
    wq = jnp.pad(wq, ((0, 0), (0, 0), (0, HEAD_PAD - QK_NOPE - QK_ROPE))).reshape(Q_LORA, H * HEAD_PAD).astype(BF16)
    wkv = w_ukv[0].reshape(KV_LORA, H, QK_NOPE + V_HEAD)
    wk = wkv[:, :, :QK_NOPE].reshape(KV_LORA, H * QK_NOPE).astype(BF16)
    wv = wkv[:, :, QK_NOPE:].reshape(KV_LORA, H * V_HEAD).astype(BF16)
    wo_a = w_out[0, :CONV_DIM].astype(BF16)
    wo_b = w_out[0, CONV_DIM:].astype(BF16)
    rw = jnp.pad(router_w[0], ((0, 0), (0, LANES - N_EXPERTS)))
    cos_f, sin_a, sin_b = _rope_tables(S)

    proj = _norm_inproj(xf, norm_mix0, w_in_p)
    y_a = _gated_conv(proj, conv_w[0], seq=S)
    q, k, v = _qkv_proj(proj, q_norm, kv_norm, wq, wk, wv, cos_f, sin_a, sin_b, seq=S)
    y_b = _flash_attention(q, k, v, batch=B, seq=S)
    x1, h1 = _out_proj(y_a, y_b, wo_a, wo_b, xf, norm_ffn0)
    x2 = _dense_ffn(h1, ffn_w_gate[0].astype(BF16), ffn_w_up[0].astype(BF16), ffn_w_down[0].astype(BF16), x1)

    x3, h3, gates, sel = _pool_router(x2, norm_mix1, pool_w[0].astype(BF16), pool_scale, norm_ffn1, rw, seq=S)
    pos, pw, tiles, meta = _rank_slots(sel, gates, tme=tme)
    pos2 = pos[:, :2]
    nt = n_slots // tme
    n_tiles = tiles[0:1, 1]
    xs = _dispatch(h3, pos2, meta[0, :N_EXPERTS], meta[1, :N_EXPERTS], meta[2, :N_EXPERTS], n_tiles,
                   n_slots=n_slots, tme=tme)
    ys = _experts(xs, tiles[:nt, 0], n_tiles, moe_w_gate[0].astype(BF16), moe_w_up[0].astype(BF16),
                  moe_w_down[0].astype(BF16), tme=tme)
    out = _combine(x3, pw, final_norm.reshape(1, D), ys, pos2)
    return out.reshape(B, S, D)
```

```python
import functools

import jax
import jax.numpy as jnp
from jax import lax
from jax.experimental import pallas as pl
from jax.experimental.pallas import tpu as pltpu

F32 = jnp.float32
BF16 = jnp.bfloat16

EPS = 1e-6
CONV_DIM = 1024
MLA_HEADS = 8
Q_LORA = 512
KV_LORA = 256
QK_NOPE = 128
QK_ROPE = 64
V_HEAD = 128
ROPE_THETA = 10000.0
POOL_WINDOWS = (2, 4, 8, 16)
N_EXPERTS = 8
LOG2E = 1.4426950408889634

LANES = 128
HEAD_PAD = 256
IN_COLS_PAD = 4096
LAT_OFF = 3 * CONV_DIM
HALO = 16
VMEM_LIMIT = 56 * 1024 * 1024


def _cparams(sem):
    return pltpu.CompilerParams(dimension_semantics=sem, vmem_limit_bytes=VMEM_LIMIT)


def _rms(x, g):
    return x * lax.rsqrt(jnp.mean(x * x, axis=-1, keepdims=True) + EPS) * g


def _norm_inproj_kernel(x_ref, g_ref, w_ref, o_ref, h_ref):
    @pl.when(pl.program_id(1) == 0)
    def _():
        h_ref[...] = _rms(x_ref[...], g_ref[...]).astype(BF16)

    o_ref[...] = jnp.dot(h_ref[...], w_ref[...], preferred_element_type=F32).astype(o_ref.dtype)


def _norm_inproj(x, g, w, *, tm=1024, tn=1024):
    T, D = x.shape
    N = w.shape[1]
    return pl.pallas_call(
        _norm_inproj_kernel,
        grid=(T // tm, N // tn),
        in_specs=[
            pl.BlockSpec((tm, D), lambda i, j: (i, 0)),
            pl.BlockSpec((1, D), lambda i, j: (0, 0)),
            pl.BlockSpec((D, tn), lambda i, j: (0, j)),
        ],
        out_specs=pl.BlockSpec((tm, tn), lambda i, j: (i, j)),
        out_shape=jax.ShapeDtypeStruct((T, N), BF16),
        scratch_shapes=[pltpu.VMEM((tm, D), BF16)],
        compiler_params=_cparams(("parallel", "arbitrary")),
        name="norm_inproj",
    )(x, g, w)


def _conv_kernel(xin_ref, gb_ref, gc_ref, hx_ref, hc_ref, w_ref, o_ref, *, ts, seq):
    i = pl.program_id(0)
    u = gc_ref[...].astype(F32) * xin_ref[...].astype(F32)
    hu = hc_ref[...].astype(F32) * hx_ref[...].astype(F32)
    hu = jnp.where((i * ts) % seq == 0, 0.0, hu)
    hm1 = hu[HALO - 1:HALO, :]
    hm2 = hu[HALO - 2:HALO - 1, :]
    row = lax.broadcasted_iota(jnp.int32, u.shape, 0)
    u1 = jnp.where(row == 0, hm1, pltpu.roll(u, 1, 0))
    u2 = jnp.where(row == 0, hm2, jnp.where(row == 1, hm1, pltpu.roll(u, 2, 0)))
    w = w_ref[...]
    y = w[0:1, :] * u2 + w[1:2, :] * u1 + w[2:3, :] * u
    o_ref[...] = (gb_ref[...].astype(F32) * y).astype(o_ref.dtype)


def _gated_conv(proj, conv_w, *, seq, ts=512):
    T = proj.shape[0]
    C = CONV_DIM
    hb = ts // HALO
    halo_map = lambda c: (lambda i: (jnp.maximum(i * hb - 1, 0), c))
    return pl.pallas_call(
        functools.partial(_conv_kernel, ts=ts, seq=seq),
        grid=(T // ts,),
        in_specs=[
            pl.BlockSpec((ts, C), lambda i: (i, 0)),
            pl.BlockSpec((ts, C), lambda i: (i, 1)),
            pl.BlockSpec((ts, C), lambda i: (i, 2)),
            pl.BlockSpec((HALO, C), halo_map(0)),
            pl.BlockSpec((HALO, C), halo_map(2)),
            pl.BlockSpec(conv_w.shape, lambda i: (0, 0)),
        ],
        out_specs=pl.BlockSpec((ts, C), lambda i: (i, 0)),
        out_shape=jax.ShapeDtypeStruct((T, C), BF16),
        compiler_params=_cparams(("parallel",)),
        name="gated_conv",
    )(proj, proj, proj, proj, proj, conv_w)


def _rope128(blk, cos_f, sin_a, sin_b):
    return blk * cos_f + pltpu.roll(blk, 96, 1) * sin_a + pltpu.roll(blk, 32, 1) * sin_b


def _qkv_kernel(lat_ref, qn_ref, kvn_ref, wq_ref, wk_ref, wv_ref, cos_ref, sa_ref, sb_ref,
                q_ref, k_ref, v_ref, *, scale):
    lat = lat_ref[...].astype(F32)
    cq = _rms(lat[:, :Q_LORA], qn_ref[...]).astype(BF16)
    ckv = _rms(lat[:, Q_LORA:Q_LORA + KV_LORA], kvn_ref[...]).astype(BF16)
    kr = lat[:, Q_LORA + KV_LORA:Q_LORA + KV_LORA + LANES]
    cos_f, sin_a, sin_b = cos_ref[...], sa_ref[...], sb_ref[...]
    q = jnp.dot(cq, wq_ref[...], preferred_element_type=F32) * scale
    kn = jnp.dot(ckv, wk_ref[...], preferred_element_type=F32)
    vv = jnp.dot(ckv, wv_ref[...], preferred_element_type=F32)
    kr_rot = _rope128(kr, cos_f, sin_a, sin_b).astype(k_ref.dtype)
    ones = jnp.ones((lat.shape[0], V_HEAD), v_ref.dtype)
    for h in range(MLA_HEADS):
        o = h * HEAD_PAD
        v_ref[:, o:o + V_HEAD] = vv[:, h * V_HEAD:(h + 1) * V_HEAD].astype(v_ref.dtype)
        v_ref[:, o + V_HEAD:o + HEAD_PAD] = ones
        q_ref[:, o:o + QK_NOPE] = q[:, o:o + QK_NOPE].astype(q_ref.dtype)
        q_ref[:, o + QK_NOPE:o + HEAD_PAD] = _rope128(
            q[:, o + QK_NOPE:o + HEAD_PAD], cos_f, sin_a, sin_b).astype(q_ref.dtype)
        k_ref[:, o:o + QK_NOPE] = kn[:, h * QK_NOPE:(h + 1) * QK_NOPE].astype(k_ref.dtype)
        k_ref[:, o + QK_NOPE:o + HEAD_PAD] = kr_rot


def _qkv_proj(proj, q_norm, kv_norm, wq, wk, wv, cos_f, sin_a, sin_b, *, seq, tm=512):
    T = proj.shape[0]
    H = MLA_HEADS
    lat_blk = LAT_OFF // 1024
    nseq = seq // tm
    tab = pl.BlockSpec((tm, LANES), lambda i: (i % nseq, 0))
    full = lambda a: pl.BlockSpec(a.shape, lambda i: (0, 0))
    return pl.pallas_call(
        functools.partial(_qkv_kernel, scale=float((QK_NOPE + QK_ROPE) ** -0.5 * LOG2E)),
        grid=(T // tm,),
        in_specs=[
            pl.BlockSpec((tm, 1024), lambda i: (i, lat_blk)),
            full(q_norm), full(kv_norm), full(wq), full(wk), full(wv),
            tab, tab, tab,
        ],
        out_specs=[
            pl.BlockSpec((tm, H * HEAD_PAD), lambda i: (i, 0)),
            pl.BlockSpec((tm, H * HEAD_PAD), lambda i: (i, 0)),
            pl.BlockSpec((tm, H * HEAD_PAD), lambda i: (i, 0)),
        ],
        out_shape=[jax.ShapeDtypeStruct((T, H * HEAD_PAD), BF16)] * 3,
        compiler_params=_cparams(("parallel",)),
        name="qkv_proj",
    )(proj, q_norm, kv_norm, wq, wk, wv, cos_f, sin_a, sin_b)


def _flash_kernel(q_ref, k_ref, v_ref, o_ref, m_ref, acc_ref, *, tq, hp):
    qi = pl.program_id(2)
    m_ref[...] = jnp.full(m_ref.shape, -jnp.inf, F32)
    acc_ref[...] = jnp.zeros(acc_ref.shape, F32)
    nc = tq // LANES

    def block(j, masked):
        start = pl.multiple_of(j * tq, tq)
        for h in range(hp):
            hs = slice(h * HEAD_PAD, (h + 1) * HEAD_PAD)
            q = q_ref[:, hs]
            k = k_ref[pl.ds(start, tq), hs]
            v = v_ref[pl.ds(start, tq), hs]
            s = lax.dot_general(q, k, (((1,), (1,)), ((), ())), preferred_element_type=F32)
            if masked:
                row = lax.broadcasted_iota(jnp.int32, s.shape, 0)
                col = lax.broadcasted_iota(jnp.int32, s.shape, 1)
                s = jnp.where(col <= row, s, -jnp.inf)
            cols = [s[:, c * LANES:(c + 1) * LANES] for c in range(nc)]
            mx = functools.reduce(jnp.maximum, cols)
            m_prev = m_ref[h]
            m_new = jnp.maximum(m_prev, jnp.broadcast_to(jnp.max(mx, axis=-1, keepdims=True), m_prev.shape))
            alpha = jnp.exp2(m_prev - m_new)
            p = jnp.concatenate([jnp.exp2(c - m_new) for c in cols], axis=1).astype(v.dtype)
            pv = jnp.dot(p, v, preferred_element_type=F32)
            acc_ref[h] = jnp.concatenate([alpha, alpha], axis=1) * acc_ref[h] + pv
            m_ref[h] = m_new

    def body(j, c):
        block(j, False)
        return c

    lax.fori_loop(0, qi, body, 0)
    block(qi, True)
    for h in range(hp):
        acc = acc_ref[h]
        o_ref[:, h * V_HEAD:(h + 1) * V_HEAD] = (acc[:, :V_HEAD] / acc[:, V_HEAD:]).astype(o_ref.dtype)


def _flash_attention(q, k, v, *, batch, seq, tq=512, hp=4):
    T = q.shape[0]
    H = MLA_HEADS
    nq = seq // tq
    return pl.pallas_call(
        functools.partial(_flash_kernel, tq=tq, hp=hp),
        grid=(batch, H // hp, nq),
        in_specs=[
            pl.BlockSpec((tq, hp * HEAD_PAD), lambda b, h, i: (b * nq + i, h)),
            pl.BlockSpec((seq, hp * HEAD_PAD), lambda b, h, i: (b, h)),
            pl.BlockSpec((seq, hp * HEAD_PAD), lambda b, h, i: (b, h)),
        ],
        out_specs=pl.BlockSpec((tq, hp * V_HEAD), lambda b, h, i: (b * nq + i, h)),
        out_shape=jax.ShapeDtypeStruct((T, H * V_HEAD), BF16),
        scratch_shapes=[
            pltpu.VMEM((hp, tq, LANES), F32),
            pltpu.VMEM((hp, tq, HEAD_PAD), F32),
        ],
        compiler_params=_cparams(("parallel", "parallel", "arbitrary")),
        name="flash_attention",
    )(q, k, v)


def _outproj_kernel(ya_ref, yb_ref, wa_ref, wb_ref, x_ref, g_ref, x1_ref, h1_ref):
    acc = jnp.dot(ya_ref[...], wa_ref[...], preferred_element_type=F32)
    acc = acc + jnp.dot(yb_ref[...], wb_ref[...], preferred_element_type=F32)
    x1 = x_ref[...] + acc
    x1_ref[...] = x1
    h1_ref[...] = _rms(x1, g_ref[...]).astype(h1_ref.dtype)


def _out_proj(ya, yb, wa, wb, x, g, *, tm=512):
    T, D = x.shape
    full = lambda a: pl.BlockSpec(a.shape, lambda i: (0, 0))
    return pl.pallas_call(
        _outproj_kernel,
        grid=(T // tm,),
        in_specs=[
            pl.BlockSpec((tm, ya.shape[1]), lambda i: (i, 0)),
            pl.BlockSpec((tm, yb.shape[1]), lambda i: (i, 0)),
            full(wa), full(wb),
            pl.BlockSpec((tm, D), lambda i: (i, 0)),
            full(g),
        ],
        out_specs=[pl.BlockSpec((tm, D), lambda i: (i, 0)), pl.BlockSpec((tm, D), lambda i: (i, 0))],
        out_shape=[jax.ShapeDtypeStruct((T, D), F32), jax.ShapeDtypeStruct((T, D), BF16)],
        compiler_params=_cparams(("parallel",)),
        name="out_proj",
    )(ya, yb, wa, wb, x, g)


def _ffn_kernel(h_ref, wg_ref, wu_ref, wd_ref, x_ref, o_ref):
    k = pl.program_id(1)

    @pl.when(k == 0)
    def _():
        o_ref[...] = x_ref[...]

    h = h_ref[...]
    g = jnp.dot(h, wg_ref[0], preferred_element_type=F32)
    u = jnp.dot(h, wu_ref[0], preferred_element_type=F32)
    a = (g * jax.nn.sigmoid(g) * u).astype(h.dtype)
    o_ref[...] += jnp.dot(a, wd_ref[...], preferred_element_type=F32)


def _chunk_major(w, tf):
    *lead, d, f = w.shape
    return jnp.swapaxes(w.reshape(*lead, d, f // tf, tf), -3, -2)


def _dense_ffn(h, wg, wu, wd, x, *, tm=512):
    T, D = x.shape
    nk, _, tf = wg.shape
    return pl.pallas_call(
        _ffn_kernel,
        grid=(T // tm, nk),
        in_specs=[
            pl.BlockSpec((tm, D), lambda i, k: (i, 0)),
            pl.BlockSpec((1, D, tf), lambda i, k: (k, 0, 0)),
            pl.BlockSpec((1, D, tf), lambda i, k: (k, 0, 0)),
            pl.BlockSpec((tf, D), lambda i, k: (k, 0)),
            pl.BlockSpec((tm, D), lambda i, k: (i, 0)),
        ],
        out_specs=pl.BlockSpec((tm, D), lambda i, k: (i, 0)),
        out_shape=jax.ShapeDtypeStruct((T, D), F32),
        compiler_params=_cparams(("parallel", "arbitrary")),
        name="dense_ffn",
    )(h, wg, wu, wd, x)


def _pool_router_kernel(x_ref, halo_ref, g1_ref, pw_ref, ps_ref, g2_ref, rw_ref,
                        x3_ref, h3_ref, gates_ref, sel_ref, *, tm, seq):
    i = pl.program_id(0)
    x = x_ref[...]
    g1 = g1_ref[...]
    h = _rms(x, g1)
    hh = _rms(halo_ref[...], g1)
    t0 = (i * tm) % seq
    hh = jnp.where(t0 == 0, 0.0, hh)
    tpos = t0 + lax.broadcasted_iota(jnp.int32, (tm, 1), 0)
    pd = x.shape[1] // len(POOL_WINDOWS)
    for g, w in enumerate(POOL_WINDOWS):
        sl = slice(g * pd, (g + 1) * pd)
        hg = h[:, sl]
        cur = jnp.concatenate([hh[:, sl], hg], axis=0)
        sh = 1
        while sh < w:
            cur = cur + pltpu.roll(cur, sh, 0)
            sh *= 2
        wsum = cur[HALO:, :]
        inv_cnt = 1.0 / jnp.minimum(tpos + 1, w).astype(F32)
        pooled = (wsum * inv_cnt - hg).astype(BF16)
        yg = jnp.dot(pooled, pw_ref[g], preferred_element_type=F32)
        x3_ref[:, sl] = x[:, sl] + yg * ps_ref[:, sl]
    x3 = x3_ref[...]
    h3 = _rms(x3, g2_ref[...])
    h3_ref[...] = h3
    h_hi = h3.astype(BF16)
    h_lo = (h3 - h_hi.astype(F32)).astype(BF16)
    rw = rw_ref[...]
    w_hi = rw.astype(BF16)
    w_lo = (rw - w_hi.astype(F32)).astype(BF16)
    logits = (jnp.dot(h_hi, w_hi, preferred_element_type=F32)
              + jnp.dot(h_lo, w_hi, preferred_element_type=F32)
              + jnp.dot(h_hi, w_lo, preferred_element_type=F32))
    lane = lax.broadcasted_iota(jnp.int32, logits.shape, 1)
    logits = jnp.where(lane < N_EXPERTS, logits, -jnp.inf)
    m1 = jnp.max(logits, axis=-1, keepdims=True)
    i1 = jnp.min(jnp.where(logits == m1, lane, LANES), axis=-1, keepdims=True)
    is1 = lane == i1
    rest = jnp.where(is1, -jnp.inf, logits)
    m2 = jnp.max(rest, axis=-1, keepdims=True)
    i2 = jnp.min(jnp.where(rest == m2, lane, LANES), axis=-1, keepdims=True)
    is2 = lane == i2
    e2 = jnp.exp(m2 - m1)
    p1 = 1.0 / (1.0 + e2)
    p2 = e2 / (1.0 + e2)
    gates_ref[...] = jnp.where(is1, p1, jnp.where(is2, p2, 0.0))
    sel_ref[...] = jnp.where(is1 | is2, 1.0, 0.0)


def _pool_router(x, g1, pool_w, pool_scale, g2, router_w, *, seq, tm=256):
    T, D = x.shape
    hb = tm // HALO
    full2 = lambda a: pl.BlockSpec(a.shape, lambda i: (0, 0))
    row_spec = lambda n: pl.BlockSpec((tm, n), lambda i: (i, 0))
    return pl.pallas_call(
        functools.partial(_pool_router_kernel, tm=tm, seq=seq),
        grid=(T // tm,),
        in_specs=[
            row_spec(D),
            pl.BlockSpec((HALO, D), lambda i: (jnp.maximum(i * hb - 1, 0), 0)),
            full2(g1),
            pl.BlockSpec(pool_w.shape, lambda i: (0, 0, 0)),
            full2(pool_scale), full2(g2), full2(router_w),
        ],
        out_specs=[row_spec(D), row_spec(D), row_spec(LANES), row_spec(LANES)],
        out_shape=[
            jax.ShapeDtypeStruct((T, D), F32),
            jax.ShapeDtypeStruct((T, D), F32),
            jax.ShapeDtypeStruct((T, LANES), F32),
            jax.ShapeDtypeStruct((T, LANES), F32),
        ],
        compiler_params=_cparams(("parallel",)),
        name="pool_router",
    )(x, x, g1, pool_w, pool_scale, g2, router_w)


def _rank_kernel(sel_ref, gates_ref, pos_ref, pw_ref, tile_ref, meta_ref, rank_ref, *, tb, tme):
    T = sel_ref.shape[0]
    nb = T // tb
    r = lax.broadcasted_iota(jnp.int32, (tb, tb), 0)
    c = lax.broadcasted_iota(jnp.int32, (tb, tb), 1)
    tri = jnp.where(c < r, 1.0, 0.0).astype(BF16)

    def count(b, carry):
        rows = pl.ds(pl.multiple_of(b * tb, tb), tb)
        blk = sel_ref[rows, :]
        rank_ref[rows, :] = jnp.dot(tri, blk.astype(BF16), preferred_element_type=F32) + carry
        return carry + jnp.sum(blk, axis=0, keepdims=True)

    counts = lax.fori_loop(0, nb, count, jnp.zeros((1, LANES), F32))
    padded = jnp.floor((counts + (tme - 1)) * (1.0 / tme)) * tme
    lane8 = lax.broadcasted_iota(jnp.int32, (8, LANES), 1)
    inc = jnp.broadcast_to(padded, (8, LANES))
    for sh in (1, 2, 4):
        inc = inc + jnp.where(lane8 >= sh, pltpu.roll(inc, sh, 1), 0.0)
    offs = inc[0:1, :] - padded

    def place(b, carry):
        rows = pl.ds(pl.multiple_of(b * tb, tb), tb)
        sel = sel_ref[rows, :] > 0.0
        gates = gates_ref[rows, :]
        slot = offs + rank_ref[rows, :]
        lane = lax.broadcasted_iota(jnp.int32, (tb, LANES), 1)
        first = jnp.min(jnp.where(sel, lane, LANES), axis=-1, keepdims=True)
        lo = lane == first
        hi = sel & jnp.logical_not(lo)
        pick = lambda m, a: jnp.sum(jnp.where(m, a, 0.0), axis=-1, keepdims=True)
        pos_ref[rows, :] = jnp.where(lane == 0, pick(lo, slot),
                                     jnp.where(lane == 1, pick(hi, slot), 0.0)).astype(jnp.int32)
        pw_ref[rows, :] = jnp.where(lane == 0, pick(lo, gates), jnp.where(lane == 1, pick(hi, gates), 0.0))
        return carry

    lax.fori_loop(0, nb, place, 0)

    first_slot = (lax.broadcasted_iota(jnp.int32, (LANES, LANES), 0) * tme).astype(F32)
    lane_t = lax.broadcasted_iota(jnp.int32, (LANES, LANES), 1)
    ended = jnp.where((lane_t < N_EXPERTS) & (first_slot >= inc[0:1, :]), 1.0, 0.0)
    expert = jnp.minimum(jnp.sum(ended, axis=-1, keepdims=True), N_EXPERTS - 1.0)
    total = jnp.max(inc[0:1, :], axis=-1, keepdims=True)
    tile_ref[...] = jnp.where(lane_t == 0, expert, jnp.where(lane_t == 1, total * (1.0 / tme), 0.0)
                              ).astype(jnp.int32)
    row8 = lax.broadcasted_iota(jnp.int32, (8, LANES), 0)
    meta_ref[...] = jnp.where(row8 == 0, counts, jnp.where(row8 == 1, offs, jnp.where(row8 == 2, padded, 0.0))
                              ).astype(jnp.int32)


def _rank_slots(sel, gates, *, tme, tb=512):
    T = sel.shape[0]
    vm = pl.BlockSpec(memory_space=pltpu.VMEM)
    return pl.pallas_call(
        functools.partial(_rank_kernel, tb=tb, tme=tme),
        in_specs=[vm, vm],
        out_specs=[vm, vm, vm, vm],
        out_shape=[
            jax.ShapeDtypeStruct((T, LANES), jnp.int32),
            jax.ShapeDtypeStruct((T, LANES), F32),
            jax.ShapeDtypeStruct((LANES, LANES), jnp.int32),
            jax.ShapeDtypeStruct((8, LANES), jnp.int32),
        ],
        scratch_shapes=[pltpu.VMEM((T, LANES), F32)],
        compiler_params=pltpu.CompilerParams(vmem_limit_bytes=VMEM_LIMIT),
        name="rank_slots",
    )(sel, gates)


def _row_copy(src, s, dst, d, sem):
    return pltpu.make_async_copy(src.at[pl.ds(s, 1)], dst.at[pl.ds(d, 1)], sem)


def _dispatch_kernel(cnt_ref, off_ref, pad_ref, nt_ref, pos_ref, h_ref, o_hbm, zero_ref, sem, zsem, *, td, tme):
    i = pl.program_id(0)
    zrows = zero_ref.shape[0]

    def start(r, c):
        for k in range(2):
            _row_copy(h_ref, r, o_hbm, pos_ref[0, 0, 2 * r + k], sem).start()
        return c

    lax.fori_loop(0, td, start, 0, unroll=8)

    @pl.when(i == 0)
    def _():
        zero_ref[...] = jnp.zeros(zero_ref.shape, zero_ref.dtype)
        for e in range(N_EXPERTS):
            lo = off_ref[e] + cnt_ref[e]
            hi = off_ref[e] + pad_ref[e]

            def zstart(s, c):
                _row_copy(zero_ref, 0, o_hbm, s, zsem).start()
                return c

            def zwait(s, c):
                _row_copy(zero_ref, 0, o_hbm, s, zsem).wait()
                return c

            lax.fori_loop(lo, hi, zstart, 0)
            lax.fori_loop(lo, hi, zwait, 0)

        def chunk(j):
            return pltpu.make_async_copy(zero_ref, o_hbm.at[pl.ds(pl.multiple_of(j * zrows, zrows), zrows)], zsem)

        c_lo = nt_ref[0] * (tme // zrows)
        c_hi = o_hbm.shape[0] // zrows
        lax.fori_loop(c_lo, c_hi, lambda j, c: (chunk(j).start(), c)[1], 0)
        lax.fori_loop(c_lo, c_hi, lambda j, c: (chunk(j).wait(), c)[1], 0)

    def wait(r, c):
        for k in range(2):
            _row_copy(h_ref, r, o_hbm, pos_ref[0, 0, 2 * r + k], sem).wait()
        return c

    lax.fori_loop(0, td, wait, 0, unroll=8)


def _dispatch(h, pos2, counts, offs, padded, n_tiles, *, n_slots, tme, td=256, zrows=256):
    T, D = h.shape
    nblk = T // td
    pos_blk = pos2.reshape(nblk, 1, 2 * td)
    grid_spec = pltpu.PrefetchScalarGridSpec(
        num_scalar_prefetch=4,
        grid=(nblk,),
        in_specs=[
            pl.BlockSpec((1, 1, 2 * td), lambda i, *_: (i, 0, 0), memory_space=pltpu.SMEM),
            pl.BlockSpec((td, D), lambda i, *_: (i, 0)),
        ],
        out_specs=pl.BlockSpec(memory_space=pl.ANY),
        scratch_shapes=[
            pltpu.VMEM((zrows, D), h.dtype),
            pltpu.SemaphoreType.DMA(()),
            pltpu.SemaphoreType.DMA(()),
        ],
    )
    return pl.pallas_call(
        functools.partial(_dispatch_kernel, td=td, tme=tme),
        grid_spec=grid_spec,
        out_shape=jax.ShapeDtypeStruct((n_slots, D), h.dtype),
        compiler_params=pltpu.CompilerParams(dimension_semantics=("arbitrary",), vmem_limit_bytes=VMEM_LIMIT),
        name="dispatch",
    )(counts, offs, padded, n_tiles, pos_blk, h)


def _experts_kernel(te_ref, nt_ref, x_ref, wg_ref, wu_ref, wd_ref, o_ref, xb_ref):
    i = pl.program_id(0)
    k = pl.program_id(1)

    @pl.when(i < nt_ref[0])
    def _():
        @pl.when(k == 0)
        def _():
            xb_ref[...] = x_ref[...].astype(xb_ref.dtype)
            o_ref[...] = jnp.zeros(o_ref.shape, o_ref.dtype)

        xb = xb_ref[...]
        g = jnp.dot(xb, wg_ref[0, 0], preferred_element_type=F32)
        u = jnp.dot(xb, wu_ref[0, 0], preferred_element_type=F32)
        a = (g * jax.nn.sigmoid(g) * u).astype(xb.dtype)
        o_ref[...] += jnp.dot(a, wd_ref[0], preferred_element_type=F32)

    @pl.when((i >= nt_ref[0]) & (k == 0))
    def _():
        o_ref[...] = jnp.zeros(o_ref.shape, o_ref.dtype)


def _experts(xs, tile_expert, n_tiles, wg, wu, wd, *, tme):
    NS, D = xs.shape
    _, nk, _, tf = wg.shape
    nt = NS // tme

    def row_map(i, k, te, n):
        return (jnp.minimum(i, n[0] - 1), 0)

    def chunk(i, k, n):
        return jnp.where(i < n[0], k, nk - 1)

    grid_spec = pltpu.PrefetchScalarGridSpec(
        num_scalar_prefetch=2,
        grid=(nt, nk),
        in_specs=[
            pl.BlockSpec((tme, D), row_map),
            pl.BlockSpec((1, 1, D, tf), lambda i, k, te, n: (te[i], chunk(i, k, n), 0, 0)),
            pl.BlockSpec((1, 1, D, tf), lambda i, k, te, n: (te[i], chunk(i, k, n), 0, 0)),
            pl.BlockSpec((1, tf, D), lambda i, k, te, n: (te[i], chunk(i, k, n), 0)),
        ],
        out_specs=pl.BlockSpec((tme, D), lambda i, k, te, n: (i, 0)),
        scratch_shapes=[pltpu.VMEM((tme, D), BF16)],
    )
    return pl.pallas_call(
        _experts_kernel,
        grid_spec=grid_spec,
        out_shape=jax.ShapeDtypeStruct((NS, D), F32),
        compiler_params=_cparams(("arbitrary", "arbitrary")),
        name="experts",
    )(tile_expert, n_tiles, xs, wg, wu, wd)


def _combine_kernel(pos_ref, nxt_ref, x_ref, pw_ref, g_ref, y_hbm, o_ref, lo_ref, hi_ref, sem, *, tc):
    i = pl.program_id(0)
    slot = i % 2

    def copies(p_ref, s, r):
        return (_row_copy(y_hbm, p_ref[0, 0, 2 * r], lo_ref.at[s], r, sem.at[s]),
                _row_copy(y_hbm, p_ref[0, 0, 2 * r + 1], hi_ref.at[s], r, sem.at[s]))

    def issue(p_ref, s):
        def body(r, c):
            for cp in copies(p_ref, s, r):
                cp.start()
            return c
        lax.fori_loop(0, tc, body, 0, unroll=8)

    @pl.when(i == 0)
    def _():
        issue(pos_ref, 0)

    @pl.when(i + 1 < pl.num_programs(0))
    def _():
        issue(nxt_ref, 1 - slot)

    def wait(r, c):
        for cp in copies(pos_ref, slot, r):
            cp.wait()
        return c

    lax.fori_loop(0, tc, wait, 0, unroll=8)
    pw = pw_ref[...]
    y = x_ref[...] + pw[:, 0:1] * lo_ref[slot] + pw[:, 1:2] * hi_ref[slot]
    o_ref[...] = _rms(y, g_ref[...])


def _combine(x, pw, g, ys, pos2, *, tc=256):
    T, D = x.shape
    nblk = T // tc
    pos_blk = pos2.reshape(nblk, 1, 2 * tc)
    return pl.pallas_call(
        functools.partial(_combine_kernel, tc=tc),
        grid=(nblk,),
        in_specs=[
            pl.BlockSpec((1, 1, 2 * tc), lambda i: (i, 0, 0), memory_space=pltpu.SMEM),
            pl.BlockSpec((1, 1, 2 * tc), lambda i: (jnp.minimum(i + 1, nblk - 1), 0, 0), memory_space=pltpu.SMEM),
            pl.BlockSpec((tc, D), lambda i: (i, 0)),
            pl.BlockSpec((tc, LANES), lambda i: (i, 0)),
            pl.BlockSpec((1, D), lambda i: (0, 0)),
            pl.BlockSpec(memory_space=pl.ANY),
        ],
        out_specs=pl.BlockSpec((tc, D), lambda i: (i, 0)),
        out_shape=jax.ShapeDtypeStruct((T, D), F32),
        scratch_shapes=[
            pltpu.VMEM((2, tc, D), F32),
            pltpu.VMEM((2, tc, D), F32),
            pltpu.SemaphoreType.DMA((2,)),
        ],
        compiler_params=_cparams(("arbitrary",)),
        name="combine",
    )(pos_blk, pos_blk, x, pw, g, ys)


def _rope_tables(seq):
    pos = jnp.arange(seq, dtype=F32)
    inv_freq = ROPE_THETA ** (-jnp.arange(0, QK_ROPE, 2, dtype=F32) / QK_ROPE)
    ang = pos[:, None] * inv_freq[None, :]
    cos, sin = jnp.cos(ang), jnp.sin(ang)
    z32 = jnp.zeros_like(cos)
    z64 = jnp.zeros((seq, LANES - QK_ROPE), F32)
    cos_f = jnp.concatenate([cos, cos, z64], axis=1)
    sin_a = jnp.concatenate([-sin, z32, z64], axis=1)
    sin_b = jnp.concatenate([z32, sin, z64], axis=1)
    return cos_f, sin_a, sin_b


def kernel(x, norm_mix0, w_in, conv_w, q_norm, w_uq, kv_norm, w_ukv, w_out, norm_ffn0, ffn_w_gate,
           ffn_w_up, ffn_w_down, norm_mix1, pool_w, pool_scale, norm_ffn1, router_w, moe_w_gate,
           moe_w_up, moe_w_down, final_norm):
    B, S, D = x.shape
    T = B * S
    H = MLA_HEADS
    tme = 1024
    n_slots = 2 * T + N_EXPERTS * tme

    xf = x.reshape(T, D)

    w_in_p = jnp.pad(w_in[0], ((0, 0), (0, IN_COLS_PAD - w_in.shape[2]))).astype(BF16)
    wq = w_uq[0].reshape(Q_LORA, H, QK_NOPE + QK_ROPE)
    wq = jnp.pad(wq, ((0, 0), (0, 0), (0, HEAD_PAD - QK_NOPE - QK_ROPE))).reshape(Q_LORA, H * HEAD_PAD).astype(BF16)
    wkv = w_ukv[0].reshape(KV_LORA, H, QK_NOPE + V_HEAD)
    wk = wkv[:, :, :QK_NOPE].reshape(KV_LORA, H * QK_NOPE).astype(BF16)
    wv = wkv[:, :, QK_NOPE:].reshape(KV_LORA, H * V_HEAD).astype(BF16)
    wo_a = w_out[0, :CONV_DIM].astype(BF16)
    wo_b = w_out[0, CONV_DIM:].astype(BF16)
    rw = jnp.pad(router_w[0], ((0, 0), (0, LANES - N_EXPERTS)))
    cos_f, sin_a, sin_b = _rope_tables(S)

    proj = _norm_inproj(xf, norm_mix0, w_in_p)
    y_a = _gated_conv(proj, conv_w[0], seq=S)
    q, k, v = _qkv_proj(proj, q_norm, kv_norm, wq, wk, wv, cos_f, sin_a, sin_b, seq=S)
    y_b = _flash_attention(q, k, v, batch=B, seq=S)
    x1, h1 = _out_proj(y_a, y_b, wo_a, wo_b, xf, norm_ffn0)
    x2 = _dense_ffn(h1, _chunk_major(ffn_w_gate[0].astype(BF16), 512), _chunk_major(ffn_w_up[0].astype(BF16), 512),
                    ffn_w_down[0].astype(BF16), x1)

    x3, h3, gates, sel = _pool_router(x2, norm_mix1, pool_w[0].astype(BF16), pool_scale, norm_ffn1, rw, seq=S)
    pos, pw, tiles, meta = _rank_slots(sel, gates, tme=tme)
    pos2 = pos[:, :2]
    nt = n_slots // tme
    n_tiles = tiles[0:1, 1]
    xs = _dispatch(h3, pos2, meta[0, :N_EXPERTS], meta[1, :N_EXPERTS], meta[2, :N_EXPERTS], n_tiles,
                   n_slots=n_slots, tme=tme)
    ys = _experts(xs, tiles[:nt, 0], n_tiles, _chunk_major(moe_w_gate[0].astype(BF16), 256),
                  _chunk_major(moe_w_up[0].astype(BF16), 256), moe_w_down[0].astype(BF16), tme=tme)
    out = _combine(x3, pw, final_norm.reshape(1, D), ys, pos2)
    return out.reshape(B, S, D)
```

```python
import functools

import jax
import jax.numpy as jnp
from jax import lax
from jax.experimental import pallas as pl
from jax.experimental.pallas import tpu as pltpu

F32 = jnp.float32
BF16 = jnp.bfloat16

EPS = 1e-6
CONV_DIM = 1024
MLA_HEADS = 8
Q_LORA = 512
KV_LORA = 256
QK_NOPE = 128
QK_ROPE = 64
V_HEAD = 128
ROPE_THETA = 10000.0
POOL_WINDOWS = (2, 4, 8, 16)
N_EXPERTS = 8
LOG2E = 1.4426950408889634

LANES = 128
HEAD_PAD = 256
IN_COLS_PAD = 4096
LAT_OFF = 3 * CONV_DIM
HALO = 16
VMEM_LIMIT = 56 * 1024 * 1024


def _cparams(sem):
    return pltpu.CompilerParams(dimension_semantics=sem, vmem_limit_bytes=VMEM_LIMIT)


def _rms(x, g):
    return x * lax.rsqrt(jnp.mean(x * x, axis=-1, keepdims=True) + EPS) * g


def _norm_inproj_kernel(x_ref, g_ref, w_ref, o_ref, h_ref):
    @pl.when(pl.program_id(1) == 0)
    def _():
        h_ref[...] = _rms(x_ref[...], g_ref[...]).astype(BF16)

    o_ref[...] = jnp.dot(h_ref[...], w_ref[...], preferred_element_type=F32).astype(o_ref.dtype)


def _norm_inproj(x, g, w, *, tm=1024, tn=1024):
    T, D = x.shape
    N = w.shape[1]
    return pl.pallas_call(
        _norm_inproj_kernel,
        grid=(T // tm, N // tn),
        in_specs=[
            pl.BlockSpec((tm, D), lambda i, j: (i, 0)),
            pl.BlockSpec((1, D), lambda i, j: (0, 0)),
            pl.BlockSpec((D, tn), lambda i, j: (0, j)),
        ],
        out_specs=pl.BlockSpec((tm, tn), lambda i, j: (i, j)),
        out_shape=jax.ShapeDtypeStruct((T, N), BF16),
        scratch_shapes=[pltpu.VMEM((tm, D), BF16)],
        compiler_params=_cparams(("parallel", "arbitrary")),
        name="norm_inproj",
    )(x, g, w)


def _conv_kernel(xin_ref, gb_ref, gc_ref, hx_ref, hc_ref, w_ref, o_ref, *, ts, seq):
    i = pl.program_id(0)
    u = gc_ref[...].astype(F32) * xin_ref[...].astype(F32)
    hu = hc_ref[...].astype(F32) * hx_ref[...].astype(F32)
    hu = jnp.where((i * ts) % seq == 0, 0.0, hu)
    hm1 = hu[HALO - 1:HALO, :]
    hm2 = hu[HALO - 2:HALO - 1, :]
    row = lax.broadcasted_iota(jnp.int32, u.shape, 0)
    u1 = jnp.where(row == 0, hm1, pltpu.roll(u, 1, 0))
    u2 = jnp.where(row == 0, hm2, jnp.where(row == 1, hm1, pltpu.roll(u, 2, 0)))
    w = w_ref[...]
    y = w[0:1, :] * u2 + w[1:2, :] * u1 + w[2:3, :] * u
    o_ref[...] = (gb_ref[...].astype(F32) * y).astype(o_ref.dtype)


def _gated_conv(proj, conv_w, *, seq, ts=512):
    T = proj.shape[0]
    C = CONV_DIM
    hb = ts // HALO
    halo_map = lambda c: (lambda i: (jnp.maximum(i * hb - 1, 0), c))
    return pl.pallas_call(
        functools.partial(_conv_kernel, ts=ts, seq=seq),
        grid=(T // ts,),
        in_specs=[
            pl.BlockSpec((ts, C), lambda i: (i, 0)),
            pl.BlockSpec((ts, C), lambda i: (i, 1)),
            pl.BlockSpec((ts, C), lambda i: (i, 2)),
            pl.BlockSpec((HALO, C), halo_map(0)),
            pl.BlockSpec((HALO, C), halo_map(2)),
            pl.BlockSpec(conv_w.shape, lambda i: (0, 0)),
        ],
        out_specs=pl.BlockSpec((ts, C), lambda i: (i, 0)),
        out_shape=jax.ShapeDtypeStruct((T, C), BF16),
        compiler_params=_cparams(("parallel",)),
        name="gated_conv",
    )(proj, proj, proj, proj, proj, conv_w)


def _rope128(blk, cos_f, sin_a, sin_b):
    return blk * cos_f + pltpu.roll(blk, 96, 1) * sin_a + pltpu.roll(blk, 32, 1) * sin_b


def _qkv_kernel(lat_ref, qn_ref, kvn_ref, wq_ref, wk_ref, wv_ref, cos_ref, sa_ref, sb_ref,
                q_ref, k_ref, v_ref, *, scale):
    lat = lat_ref[...].astype(F32)
    cq = _rms(lat[:, :Q_LORA], qn_ref[...]).astype(BF16)
    ckv = _rms(lat[:, Q_LORA:Q_LORA + KV_LORA], kvn_ref[...]).astype(BF16)
    kr = lat[:, Q_LORA + KV_LORA:Q_LORA + KV_LORA + LANES]
    cos_f, sin_a, sin_b = cos_ref[...], sa_ref[...], sb_ref[...]
    q = jnp.dot(cq, wq_ref[...], preferred_element_type=F32) * scale
    kn = jnp.dot(ckv, wk_ref[...], preferred_element_type=F32)
    vv = jnp.dot(ckv, wv_ref[...], preferred_element_type=F32)
    kr_rot = _rope128(kr, cos_f, sin_a, sin_b).astype(k_ref.dtype)
    ones = jnp.ones((lat.shape[0], V_HEAD), v_ref.dtype)
    for h in range(MLA_HEADS):
        o = h * HEAD_PAD
        v_ref[:, o:o + V_HEAD] = vv[:, h * V_HEAD:(h + 1) * V_HEAD].astype(v_ref.dtype)
        v_ref[:, o + V_HEAD:o + HEAD_PAD] = ones
        q_ref[:, o:o + QK_NOPE] = q[:, o:o + QK_NOPE].astype(q_ref.dtype)
        q_ref[:, o + QK_NOPE:o + HEAD_PAD] = _rope128(
            q[:, o + QK_NOPE:o + HEAD_PAD], cos_f, sin_a, sin_b).astype(q_ref.dtype)
        k_ref[:, o:o + QK_NOPE] = kn[:, h * QK_NOPE:(h + 1) * QK_NOPE].astype(k_ref.dtype)
        k_ref[:, o + QK_NOPE:o + HEAD_PAD] = kr_rot


def _qkv_proj(proj, q_norm, kv_norm, wq, wk, wv, cos_f, sin_a, sin_b, *, seq, tm=512):
    T = proj.shape[0]
    H = MLA_HEADS
    lat_blk = LAT_OFF // 1024
    nseq = seq // tm
    tab = pl.BlockSpec((tm, LANES), lambda i: (i % nseq, 0))
    full = lambda a: pl.BlockSpec(a.shape, lambda i: (0, 0))
    return pl.pallas_call(
        functools.partial(_qkv_kernel, scale=float((QK_NOPE + QK_ROPE) ** -0.5 * LOG2E)),
        grid=(T // tm,),
        in_specs=[
            pl.BlockSpec((tm, 1024), lambda i: (i, lat_blk)),
            full(q_norm), full(kv_norm), full(wq), full(wk), full(wv),
            tab, tab, tab,
        ],
        out_specs=[
            pl.BlockSpec((tm, H * HEAD_PAD), lambda i: (i, 0)),
            pl.BlockSpec((tm, H * HEAD_PAD), lambda i: (i, 0)),
            pl.BlockSpec((tm, H * HEAD_PAD), lambda i: (i, 0)),
        ],
        out_shape=[jax.ShapeDtypeStruct((T, H * HEAD_PAD), BF16)] * 3,
        compiler_params=_cparams(("parallel",)),
        name="qkv_proj",
    )(proj, q_norm, kv_norm, wq, wk, wv, cos_f, sin_a, sin_b)


def _flash_kernel(q_ref, k_ref, v_ref, o_ref, m_ref, acc_ref, *, tq, hp):
    qi = pl.program_id(2)
    m_ref[...] = jnp.full(m_ref.shape, -jnp.inf, F32)
    acc_ref[...] = jnp.zeros(acc_ref.shape, F32)
    nc = tq // LANES

    def block(j, masked):
        start = pl.multiple_of(j * tq, tq)
        for h in range(hp):
            hs = slice(h * HEAD_PAD, (h + 1) * HEAD_PAD)
            q = q_ref[:, hs]
            k = k_ref[pl.ds(start, tq), hs]
            v = v_ref[pl.ds(start, tq), hs]
            s = lax.dot_general(q, k, (((1,), (1,)), ((), ())), preferred_element_type=F32)
            if masked:
                row = lax.broadcasted_iota(jnp.int32, s.shape, 0)
                col = lax.broadcasted_iota(jnp.int32, s.shape, 1)
                s = jnp.where(col <= row, s, -jnp.inf)
            cols = [s[:, c * LANES:(c + 1) * LANES] for c in range(nc)]
            mx = functools.reduce(jnp.maximum, cols)
            m_prev = m_ref[h]
            m_new = jnp.maximum(m_prev, jnp.broadcast_to(jnp.max(mx, axis=-1, keepdims=True), m_prev.shape))
            alpha = jnp.exp2(m_prev - m_new)
            p = jnp.concatenate([jnp.exp2(c - m_new) for c in cols], axis=1).astype(v.dtype)
            pv = jnp.dot(p, v, preferred_element_type=F32)
            acc_ref[h] = jnp.concatenate([alpha, alpha], axis=1) * acc_ref[h] + pv
            m_ref[h] = m_new

    def body(j, c):
        block(j, False)
        return c

    lax.fori_loop(0, qi, body, 0)
    block(qi, True)
    for h in range(hp):
        acc = acc_ref[h]
        o_ref[:, h * V_HEAD:(h + 1) * V_HEAD] = (acc[:, :V_HEAD] / acc[:, V_HEAD:]).astype(o_ref.dtype)


def _flash_attention(q, k, v, *, batch, seq, tq=512, hp=4):
    T = q.shape[0]
    H = MLA_HEADS
    nq = seq // tq
    return pl.pallas_call(
        functools.partial(_flash_kernel, tq=tq, hp=hp),
        grid=(batch, H // hp, nq),
        in_specs=[
            pl.BlockSpec((tq, hp * HEAD_PAD), lambda b, h, i: (b * nq + i, h)),
            pl.BlockSpec((seq, hp * HEAD_PAD), lambda b, h, i: (b, h)),
            pl.BlockSpec((seq, hp * HEAD_PAD), lambda b, h, i: (b, h)),
        ],
        out_specs=pl.BlockSpec((tq, hp * V_HEAD), lambda b, h, i: (b * nq + i, h)),
        out_shape=jax.ShapeDtypeStruct((T, H * V_HEAD), BF16),
        scratch_shapes=[
            pltpu.VMEM((hp, tq, LANES), F32),
            pltpu.VMEM((hp, tq, HEAD_PAD), F32),
        ],
        compiler_params=_cparams(("parallel", "parallel", "arbitrary")),
        name="flash_attention",
    )(q, k, v)


def _outproj_kernel(ya_ref, yb_ref, wa_ref, wb_ref, x_ref, x1_ref):
    acc = jnp.dot(ya_ref[...], wa_ref[...], preferred_element_type=F32)
    acc = acc + jnp.dot(yb_ref[...], wb_ref[...], preferred_element_type=F32)
    x1_ref[...] = x_ref[...] + acc


def _out_proj(ya, yb, wa, wb, x, *, tm=512):
    T, D = x.shape
    full = lambda a: pl.BlockSpec(a.shape, lambda i: (0, 0))
    return pl.pallas_call(
        _outproj_kernel,
        grid=(T // tm,),
        in_specs=[
            pl.BlockSpec((tm, ya.shape[1]), lambda i: (i, 0)),
            pl.BlockSpec((tm, yb.shape[1]), lambda i: (i, 0)),
            full(wa), full(wb),
            pl.BlockSpec((tm, D), lambda i: (i, 0)),
        ],
        out_specs=pl.BlockSpec((tm, D), lambda i: (i, 0)),
        out_shape=jax.ShapeDtypeStruct((T, D), F32),
        compiler_params=_cparams(("parallel",)),
        name="out_proj",
    )(ya, yb, wa, wb, x)


def _ffn_kernel(x_ref, g_ref, wg_ref, wu_ref, wd_ref, o_ref, h_ref):
    k = pl.program_id(1)

    @pl.when(k == 0)
    def _():
        x = x_ref[...]
        h_ref[...] = _rms(x, g_ref[...]).astype(h_ref.dtype)
        o_ref[...] = x

    h = h_ref[...]
    g = jnp.dot(h, wg_ref[...], preferred_element_type=F32)
    u = jnp.dot(h, wu_ref[...], preferred_element_type=F32)
    a = (g * jax.nn.sigmoid(g) * u).astype(h.dtype)
    o_ref[...] += jnp.dot(a, wd_ref[...], preferred_element_type=F32)


def _dense_ffn(x, g, wg, wu, wd, *, tm=1024, tf=512):
    T, D = x.shape
    F = wg.shape[1]
    return pl.pallas_call(
        _ffn_kernel,
        grid=(T // tm, F // tf),
        in_specs=[
            pl.BlockSpec((tm, D), lambda i, k: (i, 0)),
            pl.BlockSpec((1, D), lambda i, k: (0, 0)),
            pl.BlockSpec((D, tf), lambda i, k: (0, k)),
            pl.BlockSpec((D, tf), lambda i, k: (0, k)),
            pl.BlockSpec((tf, D), lambda i, k: (k, 0)),
        ],
        out_specs=pl.BlockSpec((tm, D), lambda i, k: (i, 0)),
        out_shape=jax.ShapeDtypeStruct((T, D), F32),
        scratch_shapes=[pltpu.VMEM((tm, D), BF16)],
        compiler_params=_cparams(("parallel", "arbitrary")),
        name="dense_ffn",
    )(x, g, wg, wu, wd)


def _pool_router_kernel(x_ref, halo_ref, g1_ref, pw_ref, ps_ref, g2_ref, rw_ref,
                        x3_ref, h3_ref, gates_ref, sel_ref, *, tm, seq):
    i = pl.program_id(0)
    x = x_ref[...]
    g1 = g1_ref[...]
    h = _rms(x, g1)
    hh = _rms(halo_ref[...], g1)
    t0 = (i * tm) % seq
    hh = jnp.where(t0 == 0, 0.0, hh)
    tpos = t0 + lax.broadcasted_iota(jnp.int32, (tm, 1), 0)
    pd = x.shape[1] // len(POOL_WINDOWS)
    for g, w in enumerate(POOL_WINDOWS):
        sl = slice(g * pd, (g + 1) * pd)
        hg = h[:, sl]
        cur = jnp.concatenate([hh[:, sl], hg], axis=0)
        sh = 1
        while sh < w:
            cur = cur + pltpu.roll(cur, sh, 0)
            sh *= 2
        wsum = cur[HALO:, :]
        inv_cnt = 1.0 / jnp.minimum(tpos + 1, w).astype(F32)
        pooled = (wsum * inv_cnt - hg).astype(BF16)
        yg = jnp.dot(pooled, pw_ref[g], preferred_element_type=F32)
        x3_ref[:, sl] = x[:, sl] + yg * ps_ref[:, sl]
    x3 = x3_ref[...]
    h3 = _rms(x3, g2_ref[...])
    h3_ref[...] = h3
    h_hi = h3.astype(BF16)
    h_lo = (h3 - h_hi.astype(F32)).astype(BF16)
    rw = rw_ref[...]
    w_hi = rw.astype(BF16)
    w_lo = (rw - w_hi.astype(F32)).astype(BF16)
    logits = (jnp.dot(h_hi, w_hi, preferred_element_type=F32)
              + jnp.dot(h_lo, w_hi, preferred_element_type=F32)
              + jnp.dot(h_hi, w_lo, preferred_element_type=F32))
    lane = lax.broadcasted_iota(jnp.int32, logits.shape, 1)
    logits = jnp.where(lane < N_EXPERTS, logits, -jnp.inf)
    m1 = jnp.max(logits, axis=-1, keepdims=True)
    i1 = jnp.min(jnp.where(logits == m1, lane, LANES), axis=-1, keepdims=True)
    is1 = lane == i1
    rest = jnp.where(is1, -jnp.inf, logits)
    m2 = jnp.max(rest, axis=-1, keepdims=True)
    i2 = jnp.min(jnp.where(rest == m2, lane, LANES), axis=-1, keepdims=True)
    is2 = lane == i2
    e2 = jnp.exp(m2 - m1)
    p1 = 1.0 / (1.0 + e2)
    p2 = e2 / (1.0 + e2)
    gates_ref[...] = jnp.where(is1, p1, jnp.where(is2, p2, 0.0))
    sel_ref[...] = jnp.where(is1 | is2, 1.0, 0.0)


def _pool_router(x, g1, pool_w, pool_scale, g2, router_w, *, seq, tm=256):
    T, D = x.shape
    hb = tm // HALO
    full2 = lambda a: pl.BlockSpec(a.shape, lambda i: (0, 0))
    row_spec = lambda n: pl.BlockSpec((tm, n), lambda i: (i, 0))
    return pl.pallas_call(
        functools.partial(_pool_router_kernel, tm=tm, seq=seq),
        grid=(T // tm,),
        in_specs=[
            row_spec(D),
            pl.BlockSpec((HALO, D), lambda i: (jnp.maximum(i * hb - 1, 0), 0)),
            full2(g1),
            pl.BlockSpec(pool_w.shape, lambda i: (0, 0, 0)),
            full2(pool_scale), full2(g2), full2(router_w),
        ],
        out_specs=[row_spec(D), row_spec(D), row_spec(LANES), row_spec(LANES)],
        out_shape=[
            jax.ShapeDtypeStruct((T, D), F32),
            jax.ShapeDtypeStruct((T, D), F32),
            jax.ShapeDtypeStruct((T, LANES), F32),
            jax.ShapeDtypeStruct((T, LANES), F32),
        ],
        compiler_params=_cparams(("parallel",)),
        name="pool_router",
    )(x, x, g1, pool_w, pool_scale, g2, router_w)


def _rank_kernel(sel_ref, gates_ref, pos_ref, pw_ref, tile_ref, meta_ref, rank_ref, *, tb, tme):
    T = sel_ref.shape[0]
    nb = T // tb
    r = lax.broadcasted_iota(jnp.int32, (tb, tb), 0)
    c = lax.broadcasted_iota(jnp.int32, (tb, tb), 1)
    tri = jnp.where(c < r, 1.0, 0.0).astype(BF16)

    def count(b, carry):
        rows = pl.ds(pl.multiple_of(b * tb, tb), tb)
        blk = sel_ref[rows, :]
        rank_ref[rows, :] = jnp.dot(tri, blk.astype(BF16), preferred_element_type=F32) + carry
        return carry + jnp.sum(blk, axis=0, keepdims=True)

    counts = lax.fori_loop(0, nb, count, jnp.zeros((1, LANES), F32))
    padded = jnp.floor((counts + (tme - 1)) * (1.0 / tme)) * tme
    lane8 = lax.broadcasted_iota(jnp.int32, (8, LANES), 1)
    inc = jnp.broadcast_to(padded, (8, LANES))
    for sh in (1, 2, 4):
        inc = inc + jnp.where(lane8 >= sh, pltpu.roll(inc, sh, 1), 0.0)
    offs = inc[0:1, :] - padded

    def place(b, carry):
        rows = pl.ds(pl.multiple_of(b * tb, tb), tb)
        sel = sel_ref[rows, :] > 0.0
        gates = gates_ref[rows, :]
        slot = offs + rank_ref[rows, :]
        lane = lax.broadcasted_iota(jnp.int32, (tb, LANES), 1)
        first = jnp.min(jnp.where(sel, lane, LANES), axis=-1, keepdims=True)
        lo = lane == first
        hi = sel & jnp.logical_not(lo)
        pick = lambda m, a: jnp.sum(jnp.where(m, a, 0.0), axis=-1, keepdims=True)
        pos_ref[rows, :] = jnp.where(lane == 0, pick(lo, slot),
                                     jnp.where(lane == 1, pick(hi, slot), 0.0)).astype(jnp.int32)
        pw_ref[rows, :] = jnp.where(lane == 0, pick(lo, gates), jnp.where(lane == 1, pick(hi, gates), 0.0))
        return carry

    lax.fori_loop(0, nb, place, 0)

    first_slot = (lax.broadcasted_iota(jnp.int32, (LANES, LANES), 0) * tme).astype(F32)
    lane_t = lax.broadcasted_iota(jnp.int32, (LANES, LANES), 1)
    ended = jnp.where((lane_t < N_EXPERTS) & (first_slot >= inc[0:1, :]), 1.0, 0.0)
    expert = jnp.minimum(jnp.sum(ended, axis=-1, keepdims=True), N_EXPERTS - 1.0)
    total = jnp.max(inc[0:1, :], axis=-1, keepdims=True)
    tile_ref[...] = jnp.where(lane_t == 0, expert, jnp.where(lane_t == 1, total * (1.0 / tme), 0.0)
                              ).astype(jnp.int32)
    row8 = lax.broadcasted_iota(jnp.int32, (8, LANES), 0)
    meta_ref[...] = jnp.where(row8 == 0, counts, jnp.where(row8 == 1, offs, jnp.where(row8 == 2, padded, 0.0))
                              ).astype(jnp.int32)


def _rank_slots(sel, gates, *, tme, tb=512):
    T = sel.shape[0]
    vm = pl.BlockSpec(memory_space=pltpu.VMEM)
    return pl.pallas_call(
        functools.partial(_rank_kernel, tb=tb, tme=tme),
        in_specs=[vm, vm],
        out_specs=[vm, vm, vm, vm],
        out_shape=[
            jax.ShapeDtypeStruct((T, LANES), jnp.int32),
            jax.ShapeDtypeStruct((T, LANES), F32),
            jax.ShapeDtypeStruct((LANES, LANES), jnp.int32),
            jax.ShapeDtypeStruct((8, LANES), jnp.int32),
        ],
        scratch_shapes=[pltpu.VMEM((T, LANES), F32)],
        compiler_params=pltpu.CompilerParams(vmem_limit_bytes=VMEM_LIMIT),
        name="rank_slots",
    )(sel, gates)


def _row_copy(src, s, dst, d, sem):
    return pltpu.make_async_copy(src.at[pl.ds(s, 1)], dst.at[pl.ds(d, 1)], sem)


def _dispatch_kernel(cnt_ref, off_ref, pad_ref, nt_ref, pos_ref, h_ref, o_hbm, zero_ref, sem, zsem, *, td, tme):
    i = pl.program_id(0)
    zrows = zero_ref.shape[0]

    def start(r, c):
        for k in range(2):
            _row_copy(h_ref, r, o_hbm, pos_ref[0, 0, 2 * r + k], sem).start()
        return c

    lax.fori_loop(0, td, start, 0, unroll=8)

    @pl.when(i == 0)
    def _():
        zero_ref[...] = jnp.zeros(zero_ref.shape, zero_ref.dtype)
        for e in range(N_EXPERTS):
            lo = off_ref[e] + cnt_ref[e]
            hi = off_ref[e] + pad_ref[e]

            def zstart(s, c):
                _row_copy(zero_ref, 0, o_hbm, s, zsem).start()
                return c

            def zwait(s, c):
                _row_copy(zero_ref, 0, o_hbm, s, zsem).wait()
                return c

            lax.fori_loop(lo, hi, zstart, 0)
            lax.fori_loop(lo, hi, zwait, 0)

        def chunk(j):
            return pltpu.make_async_copy(zero_ref, o_hbm.at[pl.ds(pl.multiple_of(j * zrows, zrows), zrows)], zsem)

        c_lo = nt_ref[0] * (tme // zrows)
        c_hi = o_hbm.shape[0] // zrows
        lax.fori_loop(c_lo, c_hi, lambda j, c: (chunk(j).start(), c)[1], 0)
        lax.fori_loop(c_lo, c_hi, lambda j, c: (chunk(j).wait(), c)[1], 0)

    def wait(r, c):
        for k in range(2):
            _row_copy(h_ref, r, o_hbm, pos_ref[0, 0, 2 * r + k], sem).wait()
        return c

    lax.fori_loop(0, td, wait, 0, unroll=8)


def _dispatch(h, pos2, counts, offs, padded, n_tiles, *, n_slots, tme, td=256, zrows=256):
    T, D = h.shape
    nblk = T // td
    pos_blk = pos2.reshape(nblk, 1, 2 * td)
    grid_spec = pltpu.PrefetchScalarGridSpec(
        num_scalar_prefetch=4,
        grid=(nblk,),
        in_specs=[
            pl.BlockSpec((1, 1, 2 * td), lambda i, *_: (i, 0, 0), memory_space=pltpu.SMEM),
            pl.BlockSpec((td, D), lambda i, *_: (i, 0)),
        ],
        out_specs=pl.BlockSpec(memory_space=pl.ANY),
        scratch_shapes=[
            pltpu.VMEM((zrows, D), h.dtype),
            pltpu.SemaphoreType.DMA(()),
            pltpu.SemaphoreType.DMA(()),
        ],
    )
    return pl.pallas_call(
        functools.partial(_dispatch_kernel, td=td, tme=tme),
        grid_spec=grid_spec,
        out_shape=jax.ShapeDtypeStruct((n_slots, D), h.dtype),
        compiler_params=pltpu.CompilerParams(dimension_semantics=("arbitrary",), vmem_limit_bytes=VMEM_LIMIT),
        name="dispatch",
    )(counts, offs, padded, n_tiles, pos_blk, h)


def _experts_kernel(te_ref, nt_ref, x_ref, wg_ref, wu_ref, wd_ref, o_ref, xb_ref):
    i = pl.program_id(0)
    k = pl.program_id(1)

    @pl.when(i < nt_ref[0])
    def _():
        @pl.when(k == 0)
        def _():
            xb_ref[...] = x_ref[...].astype(xb_ref.dtype)
            o_ref[...] = jnp.zeros(o_ref.shape, o_ref.dtype)

        xb = xb_ref[...]
        cast = lambda w_ref: w_ref[0].astype(xb.dtype)
        g = jnp.dot(xb, cast(wg_ref), preferred_element_type=F32)
        u = jnp.dot(xb, cast(wu_ref), preferred_element_type=F32)
        a = (g * jax.nn.sigmoid(g) * u).astype(xb.dtype)
        o_ref[...] += jnp.dot(a, cast(wd_ref), preferred_element_type=F32)

    @pl.when((i >= nt_ref[0]) & (k == 0))
    def _():
        o_ref[...] = jnp.zeros(o_ref.shape, o_ref.dtype)


def _experts(xs, tile_expert, n_tiles, wg, wu, wd, *, tme, tf=256):
    NS, D = xs.shape
    nk = wg.shape[2] // tf
    nt = NS // tme

    def row_map(i, k, te, n):
        return (jnp.minimum(i, n[0] - 1), 0)

    def chunk(i, k, n):
        return jnp.where(i < n[0], k, nk - 1)

    grid_spec = pltpu.PrefetchScalarGridSpec(
        num_scalar_prefetch=2,
        grid=(nt, nk),
        in_specs=[
            pl.BlockSpec((tme, D), row_map),
            pl.BlockSpec((1, D, tf), lambda i, k, te, n: (te[i], 0, chunk(i, k, n))),
            pl.BlockSpec((1, D, tf), lambda i, k, te, n: (te[i], 0, chunk(i, k, n))),
            pl.BlockSpec((1, tf, D), lambda i, k, te, n: (te[i], chunk(i, k, n), 0)),
        ],
        out_specs=pl.BlockSpec((tme, D), lambda i, k, te, n: (i, 0)),
        scratch_shapes=[pltpu.VMEM((tme, D), BF16)],
    )
    return pl.pallas_call(
        _experts_kernel,
        grid_spec=grid_spec,
        out_shape=jax.ShapeDtypeStruct((NS, D), F32),
        compiler_params=_cparams(("arbitrary", "arbitrary")),
        name="experts",
    )(tile_expert, n_tiles, xs, wg, wu, wd)


def _combine_kernel(pos_ref, nxt_ref, x_ref, pw_ref, g_ref, y_hbm, o_ref, lo_ref, hi_ref, sem, *, tc):
    i = pl.program_id(0)
    slot = i % 2

    def copies(p_ref, s, r):
        return (_row_copy(y_hbm, p_ref[0, 0, 2 * r], lo_ref.at[s], r, sem.at[s]),
                _row_copy(y_hbm, p_ref[0, 0, 2 * r + 1], hi_ref.at[s], r, sem.at[s]))

    def issue(p_ref, s):
        def body(r, c):
            for cp in copies(p_ref, s, r):
                cp.start()
            return c
        lax.fori_loop(0, tc, body, 0, unroll=8)

    @pl.when(i == 0)
    def _():
        issue(pos_ref, 0)

    @pl.when(i + 1 < pl.num_programs(0))
    def _():
        issue(nxt_ref, 1 - slot)

    def wait(r, c):
        for cp in copies(pos_ref, slot, r):
            cp.wait()
        return c

    lax.fori_loop(0, tc, wait, 0, unroll=8)
    pw = pw_ref[...]
    y = x_ref[...] + pw[:, 0:1] * lo_ref[slot] + pw[:, 1:2] * hi_ref[slot]
    o_ref[...] = _rms(y, g_ref[...])


def _combine(x, pw, g, ys, pos2, *, tc=256):
    T, D = x.shape
    nblk = T // tc
    pos_blk = pos2.reshape(nblk, 1, 2 * tc)
    return pl.pallas_call(
        functools.partial(_combine_kernel, tc=tc),
        grid=(nblk,),
        in_specs=[
            pl.BlockSpec((1, 1, 2 * tc), lambda i: (i, 0, 0), memory_space=pltpu.SMEM),
            pl.BlockSpec((1, 1, 2 * tc), lambda i: (jnp.minimum(i + 1, nblk - 1), 0, 0), memory_space=pltpu.SMEM),
            pl.BlockSpec((tc, D), lambda i: (i, 0)),
            pl.BlockSpec((tc, LANES), lambda i: (i, 0)),
            pl.BlockSpec((1, D), lambda i: (0, 0)),
            pl.BlockSpec(memory_space=pl.ANY),
        ],
        out_specs=pl.BlockSpec((tc, D), lambda i: (i, 0)),
        out_shape=jax.ShapeDtypeStruct((T, D), F32),
        scratch_shapes=[
            pltpu.VMEM((2, tc, D), F32),
            pltpu.VMEM((2, tc, D), F32),
            pltpu.SemaphoreType.DMA((2,)),
        ],
        compiler_params=_cparams(("arbitrary",)),
        name="combine",
    )(pos_blk, pos_blk, x, pw, g, ys)


def _rope_tables(seq):
    pos = jnp.arange(seq, dtype=F32)
    inv_freq = ROPE_THETA ** (-jnp.arange(0, QK_ROPE, 2, dtype=F32) / QK_ROPE)
    ang = pos[:, None] * inv_freq[None, :]
    cos, sin = jnp.cos(ang), jnp.sin(ang)
    z32 = jnp.zeros_like(cos)
    z64 = jnp.zeros((seq, LANES - QK_ROPE), F32)
    cos_f = jnp.concatenate([cos, cos, z64], axis=1)
    sin_a = jnp.concatenate([-sin, z32, z64], axis=1)
    sin_b = jnp.concatenate([z32, sin, z64], axis=1)
    return cos_f, sin_a, sin_b


def kernel(x, norm_mix0, w_in, conv_w, q_norm, w_uq, kv_norm, w_ukv, w_out, norm_ffn0, ffn_w_gate,
           ffn_w_up, ffn_w_down, norm_mix1, pool_w, pool_scale, norm_ffn1, router_w, moe_w_gate,
           moe_w_up, moe_w_down, final_norm):
    B, S, D = x.shape
    T = B * S
    H = MLA_HEADS
    tme = 1024
    n_slots = 2 * T + N_EXPERTS * tme

    xf = x.reshape(T, D)

    w_in_p = jnp.pad(w_in[0], ((0, 0), (0, IN_COLS_PAD - w_in.shape[2]))).astype(BF16)
    wq = w_uq[0].reshape(Q_LORA, H, QK_NOPE + QK_ROPE)
    wq = jnp.pad(wq, ((0, 0), (0, 0), (0, HEAD_PAD - QK_NOPE - QK_ROPE))).reshape(Q_LORA, H * HEAD_PAD).astype(BF16)
    wkv = w_ukv[0].reshape(KV_LORA, H, QK_NOPE + V_HEAD)
    wk = wkv[:, :, :QK_NOPE].reshape(KV_LORA, H * QK_NOPE).astype(BF16)
    wv = wkv[:, :, QK_NOPE:].reshape(KV_LORA, H * V_HEAD).astype(BF16)
    wo_a = w_out[0, :CONV_DIM].astype(BF16)
    wo_b = w_out[0, CONV_DIM:].astype(BF16)
    rw = jnp.pad(router_w[0], ((0, 0), (0, LANES - N_EXPERTS)))
    cos_f, sin_a, sin_b = _rope_tables(S)

    proj = _norm_inproj(xf, norm_mix0, w_in_p)
    y_a = _gated_conv(proj, conv_w[0], seq=S)
    q, k, v = _qkv_proj(proj, q_norm, kv_norm, wq, wk, wv, cos_f, sin_a, sin_b, seq=S)
    y_b = _flash_attention(q, k, v, batch=B, seq=S)
    x1 = _out_proj(y_a, y_b, wo_a, wo_b, xf)
    x2 = _dense_ffn(x1, norm_ffn0, ffn_w_gate[0].astype(BF16), ffn_w_up[0].astype(BF16), ffn_w_down[0].astype(BF16))

    x3, h3, gates, sel = _pool_router(x2, norm_mix1, pool_w[0].astype(BF16), pool_scale, norm_ffn1, rw, seq=S)
    pos, pw, tiles, meta = _rank_slots(sel, gates, tme=tme)
    pos2 = pos[:, :2]
    nt = n_slots // tme
    n_tiles = tiles[0:1, 1]
    xs = _dispatch(h3, pos2, meta[0, :N_EXPERTS], meta[1, :N_EXPERTS], meta[2, :N_EXPERTS], n_tiles,
                   n_slots=n_slots, tme=tme)
    ys = _experts(xs, tiles[:nt, 0], n_tiles, moe_w_gate[0], moe_w_up[0], moe_w_down[0], tme=tme)
    out = _combine(x3, pw, final_norm.reshape(1, D), ys, pos2)
    return out.reshape(B, S, D)
```

```python
import functools

import jax
import jax.numpy as jnp
from jax import lax
from jax.experimental import pallas as pl
from jax.experimental.pallas import tpu as pltpu

F32 = jnp.float32
BF16 = jnp.bfloat16

EPS = 1e-6
CONV_DIM = 1024
MLA_HEADS = 8
Q_LORA = 512
KV_LORA = 256
QK_NOPE = 128
QK_ROPE = 64
V_HEAD = 128
ROPE_THETA = 10000.0
POOL_WINDOWS = (2, 4, 8, 16)
N_EXPERTS = 8
LOG2E = 1.4426950408889634

LANES = 128
HEAD_PAD = 256
IN_COLS_PAD = 4096
LAT_OFF = 3 * CONV_DIM
HALO = 16
VMEM_LIMIT = 56 * 1024 * 1024


def _cparams(sem):
    return pltpu.CompilerParams(dimension_semantics=sem, vmem_limit_bytes=VMEM_LIMIT)


def _rms(x, g):
    return x * lax.rsqrt(jnp.mean(x * x, axis=-1, keepdims=True) + EPS) * g


def _norm_inproj_kernel(x_ref, g_ref, w_ref, o_ref, h_ref):
    @pl.when(pl.program_id(1) == 0)
    def _():
        h_ref[...] = _rms(x_ref[...], g_ref[...]).astype(BF16)

    o_ref[...] = jnp.dot(h_ref[...], w_ref[...], preferred_element_type=F32).astype(o_ref.dtype)


def _norm_inproj(x, g, w, *, tm=1024, tn=1024):
    T, D = x.shape
    N = w.shape[1]
    return pl.pallas_call(
        _norm_inproj_kernel,
        grid=(T // tm, N // tn),
        in_specs=[
            pl.BlockSpec((tm, D), lambda i, j: (i, 0)),
            pl.BlockSpec((1, D), lambda i, j: (0, 0)),
            pl.BlockSpec((D, tn), lambda i, j: (0, j)),
        ],
        out_specs=pl.BlockSpec((tm, tn), lambda i, j: (i, j)),
        out_shape=jax.ShapeDtypeStruct((T, N), BF16),
        scratch_shapes=[pltpu.VMEM((tm, D), BF16)],
        compiler_params=_cparams(("parallel", "arbitrary")),
        name="norm_inproj",
    )(x, g, w)


def _conv_kernel(xin_ref, gb_ref, gc_ref, hx_ref, hc_ref, w_ref, o_ref, *, ts, seq):
    i = pl.program_id(0)
    u = gc_ref[...].astype(F32) * xin_ref[...].astype(F32)
    hu = hc_ref[...].astype(F32) * hx_ref[...].astype(F32)
    hu = jnp.where((i * ts) % seq == 0, 0.0, hu)
    hm1 = hu[HALO - 1:HALO, :]
    hm2 = hu[HALO - 2:HALO - 1, :]
    row = lax.broadcasted_iota(jnp.int32, u.shape, 0)
    u1 = jnp.where(row == 0, hm1, pltpu.roll(u, 1, 0))
    u2 = jnp.where(row == 0, hm2, jnp.where(row == 1, hm1, pltpu.roll(u, 2, 0)))
    w = w_ref[...]
    y = w[0:1, :] * u2 + w[1:2, :] * u1 + w[2:3, :] * u
    o_ref[...] = (gb_ref[...].astype(F32) * y).astype(o_ref.dtype)


def _gated_conv(proj, conv_w, *, seq, ts=512):
    T = proj.shape[0]
    C = CONV_DIM
    hb = ts // HALO
    halo_map = lambda c: (lambda i: (jnp.maximum(i * hb - 1, 0), c))
    return pl.pallas_call(
        functools.partial(_conv_kernel, ts=ts, seq=seq),
        grid=(T // ts,),
        in_specs=[
            pl.BlockSpec((ts, C), lambda i: (i, 0)),
            pl.BlockSpec((ts, C), lambda i: (i, 1)),
            pl.BlockSpec((ts, C), lambda i: (i, 2)),
            pl.BlockSpec((HALO, C), halo_map(0)),
            pl.BlockSpec((HALO, C), halo_map(2)),
            pl.BlockSpec(conv_w.shape, lambda i: (0, 0)),
        ],
        out_specs=pl.BlockSpec((ts, C), lambda i: (i, 0)),
        out_shape=jax.ShapeDtypeStruct((T, C), BF16),
        compiler_params=_cparams(("parallel",)),
        name="gated_conv",
    )(proj, proj, proj, proj, proj, conv_w)


def _rope128(blk, cos_f, sin_a, sin_b):
    return blk * cos_f + pltpu.roll(blk, 96, 1) * sin_a + pltpu.roll(blk, 32, 1) * sin_b


def _qkv_kernel(lat_ref, qn_ref, kvn_ref, wq_ref, wk_ref, wv_ref, cos_ref, sa_ref, sb_ref,
                q_ref, k_ref, v_ref, *, scale):
    lat = lat_ref[...].astype(F32)
    cq = _rms(lat[:, :Q_LORA], qn_ref[...]).astype(BF16)
    ckv = _rms(lat[:, Q_LORA:Q_LORA + KV_LORA], kvn_ref[...]).astype(BF16)
    kr = lat[:, Q_LORA + KV_LORA:Q_LORA + KV_LORA + LANES]
    cos_f, sin_a, sin_b = cos_ref[...], sa_ref[...], sb_ref[...]
    q = jnp.dot(cq, wq_ref[...], preferred_element_type=F32) * scale
    kn = jnp.dot(ckv, wk_ref[...], preferred_element_type=F32)
    vv = jnp.dot(ckv, wv_ref[...], preferred_element_type=F32)
    kr_rot = _rope128(kr, cos_f, sin_a, sin_b).astype(k_ref.dtype)
    ones = jnp.ones((lat.shape[0], V_HEAD), v_ref.dtype)
    for h in range(MLA_HEADS):
        o = h * HEAD_PAD
        v_ref[:, o:o + V_HEAD] = vv[:, h * V_HEAD:(h + 1) * V_HEAD].astype(v_ref.dtype)
        v_ref[:, o + V_HEAD:o + HEAD_PAD] = ones
        q_ref[:, o:o + QK_NOPE] = q[:, o:o + QK_NOPE].astype(q_ref.dtype)
        q_ref[:, o + QK_NOPE:o + HEAD_PAD] = _rope128(
            q[:, o + QK_NOPE:o + HEAD_PAD], cos_f, sin_a, sin_b).astype(q_ref.dtype)
        k_ref[:, o:o + QK_NOPE] = kn[:, h * QK_NOPE:(h + 1) * QK_NOPE].astype(k_ref.dtype)
        k_ref[:, o + QK_NOPE:o + HEAD_PAD] = kr_rot


def _qkv_proj(proj, q_norm, kv_norm, wq, wk, wv, cos_f, sin_a, sin_b, *, seq, tm=512):
    T = proj.shape[0]
    H = MLA_HEADS
    lat_blk = LAT_OFF // 1024
    nseq = seq // tm
    tab = pl.BlockSpec((tm, LANES), lambda i: (i % nseq, 0))
    full = lambda a: pl.BlockSpec(a.shape, lambda i: (0, 0))
    return pl.pallas_call(
        functools.partial(_qkv_kernel, scale=float((QK_NOPE + QK_ROPE) ** -0.5 * LOG2E)),
        grid=(T // tm,),
        in_specs=[
            pl.BlockSpec((tm, 1024), lambda i: (i, lat_blk)),
            full(q_norm), full(kv_norm), full(wq), full(wk), full(wv),
            tab, tab, tab,
        ],
        out_specs=[
            pl.BlockSpec((tm, H * HEAD_PAD), lambda i: (i, 0)),
            pl.BlockSpec((tm, H * HEAD_PAD), lambda i: (i, 0)),
            pl.BlockSpec((tm, H * HEAD_PAD), lambda i: (i, 0)),
        ],
        out_shape=[jax.ShapeDtypeStruct((T, H * HEAD_PAD), BF16)] * 3,
        compiler_params=_cparams(("parallel",)),
        name="qkv_proj",
    )(proj, q_norm, kv_norm, wq, wk, wv, cos_f, sin_a, sin_b)


def _flash_kernel(q_ref, k_ref, v_ref, o_ref, m_ref, acc_ref, *, tq, hp):
    qi = pl.program_id(2)
    m_ref[...] = jnp.full(m_ref.shape, -jnp.inf, F32)
    acc_ref[...] = jnp.zeros(acc_ref.shape, F32)
    nc = tq // LANES

    def block(j, masked):
        start = pl.multiple_of(j * tq, tq)
        for h in range(hp):
            hs = slice(h * HEAD_PAD, (h + 1) * HEAD_PAD)
            q = q_ref[:, hs]
            k = k_ref[pl.ds(start, tq), hs]
            v = v_ref[pl.ds(start, tq), hs]
            s = lax.dot_general(q, k, (((1,), (1,)), ((), ())), preferred_element_type=F32)
            if masked:
                row = lax.broadcasted_iota(jnp.int32, s.shape, 0)
                col = lax.broadcasted_iota(jnp.int32, s.shape, 1)
                s = jnp.where(col <= row, s, -jnp.inf)
            cols = [s[:, c * LANES:(c + 1) * LANES] for c in range(nc)]
            mx = functools.reduce(jnp.maximum, cols)
            m_prev = m_ref[h]
            m_new = jnp.maximum(m_prev, jnp.broadcast_to(jnp.max(mx, axis=-1, keepdims=True), m_prev.shape))
            alpha = jnp.exp2(m_prev - m_new)
            p = jnp.concatenate([jnp.exp2(c - m_new) for c in cols], axis=1).astype(v.dtype)
            pv = jnp.dot(p, v, preferred_element_type=F32)
            acc_ref[h] = jnp.concatenate([alpha, alpha], axis=1) * acc_ref[h] + pv
            m_ref[h] = m_new

    def body(j, c):
        block(j, False)
        return c

    lax.fori_loop(0, qi, body, 0)
    block(qi, True)
    for h in range(hp):
        acc = acc_ref[h]
        o_ref[:, h * V_HEAD:(h + 1) * V_HEAD] = (acc[:, :V_HEAD] / acc[:, V_HEAD:]).astype(o_ref.dtype)


def _flash_attention(q, k, v, *, batch, seq, tq=512, hp=4):
    T = q.shape[0]
    H = MLA_HEADS
    nq = seq // tq
    return pl.pallas_call(
        functools.partial(_flash_kernel, tq=tq, hp=hp),
        grid=(batch, H // hp, nq),
        in_specs=[
            pl.BlockSpec((tq, hp * HEAD_PAD), lambda b, h, i: (b * nq + i, h)),
            pl.BlockSpec((seq, hp * HEAD_PAD), lambda b, h, i: (b, h)),
            pl.BlockSpec((seq, hp * HEAD_PAD), lambda b, h, i: (b, h)),
        ],
        out_specs=pl.BlockSpec((tq, hp * V_HEAD), lambda b, h, i: (b * nq + i, h)),
        out_shape=jax.ShapeDtypeStruct((T, H * V_HEAD), BF16),
        scratch_shapes=[
            pltpu.VMEM((hp, tq, LANES), F32),
            pltpu.VMEM((hp, tq, HEAD_PAD), F32),
        ],
        compiler_params=_cparams(("parallel", "parallel", "arbitrary")),
        name="flash_attention",
    )(q, k, v)


def _outproj_kernel(ya_ref, yb_ref, wa_ref, wb_ref, x_ref, x1_ref):
    acc = jnp.dot(ya_ref[...], wa_ref[...], preferred_element_type=F32)
    acc = acc + jnp.dot(yb_ref[...], wb_ref[...], preferred_element_type=F32)
    x1_ref[...] = x_ref[...] + acc


def _out_proj(ya, yb, wa, wb, x, *, tm=512):
    T, D = x.shape
    full = lambda a: pl.BlockSpec(a.shape, lambda i: (0, 0))
    return pl.pallas_call(
        _outproj_kernel,
        grid=(T // tm,),
        in_specs=[
            pl.BlockSpec((tm, ya.shape[1]), lambda i: (i, 0)),
            pl.BlockSpec((tm, yb.shape[1]), lambda i: (i, 0)),
            full(wa), full(wb),
            pl.BlockSpec((tm, D), lambda i: (i, 0)),
        ],
        out_specs=pl.BlockSpec((tm, D), lambda i: (i, 0)),
        out_shape=jax.ShapeDtypeStruct((T, D), F32),
        compiler_params=_cparams(("parallel",)),
        name="out_proj",
    )(ya, yb, wa, wb, x)


def _ffn_kernel(x_ref, g_ref, wg_ref, wu_ref, wd_ref, o_ref, h_ref):
    k = pl.program_id(1)

    @pl.when(k == 0)
    def _():
        x = x_ref[...]
        h_ref[...] = _rms(x, g_ref[...]).astype(h_ref.dtype)
        o_ref[...] = x

    h = h_ref[...]
    g = jnp.dot(h, wg_ref[...], preferred_element_type=F32)
    u = jnp.dot(h, wu_ref[...], preferred_element_type=F32)
    a = (g * jax.nn.sigmoid(g) * u).astype(h.dtype)
    o_ref[...] += jnp.dot(a, wd_ref[...], preferred_element_type=F32)


def _dense_ffn(x, g, wg, wu, wd, *, tm=1024, tf=512):
    T, D = x.shape
    F = wg.shape[1]
    return pl.pallas_call(
        _ffn_kernel,
        grid=(T // tm, F // tf),
        in_specs=[
            pl.BlockSpec((tm, D), lambda i, k: (i, 0)),
            pl.BlockSpec((1, D), lambda i, k: (0, 0)),
            pl.BlockSpec((D, tf), lambda i, k: (0, k)),
            pl.BlockSpec((D, tf), lambda i, k: (0, k)),
            pl.BlockSpec((tf, D), lambda i, k: (k, 0)),
        ],
        out_specs=pl.BlockSpec((tm, D), lambda i, k: (i, 0)),
        out_shape=jax.ShapeDtypeStruct((T, D), F32),
        scratch_shapes=[pltpu.VMEM((tm, D), BF16)],
        compiler_params=_cparams(("parallel", "arbitrary")),
        name="dense_ffn",
    )(x, g, wg, wu, wd)


def _pool_router_kernel(x_ref, halo_ref, g1_ref, pw_ref, ps_ref, g2_ref, rw_ref,
                        x3_ref, h3_ref, gates_ref, sel_ref, *, tm, seq):
    i = pl.program_id(0)
    x = x_ref[...]
    g1 = g1_ref[...]
    h = _rms(x, g1)
    hh = _rms(halo_ref[...], g1)
    t0 = (i * tm) % seq
    hh = jnp.where(t0 == 0, 0.0, hh)
    tpos = t0 + lax.broadcasted_iota(jnp.int32, (tm, 1), 0)
    pd = x.shape[1] // len(POOL_WINDOWS)
    for g, w in enumerate(POOL_WINDOWS):
        sl = slice(g * pd, (g + 1) * pd)
        hg = h[:, sl]
        cur = jnp.concatenate([hh[:, sl], hg], axis=0)
        sh = 1
        while sh < w:
            cur = cur + pltpu.roll(cur, sh, 0)
            sh *= 2
        wsum = cur[HALO:, :]
        inv_cnt = 1.0 / jnp.minimum(tpos + 1, w).astype(F32)
        pooled = (wsum * inv_cnt - hg).astype(BF16)
        yg = jnp.dot(pooled, pw_ref[g], preferred_element_type=F32)
        x3_ref[:, sl] = x[:, sl] + yg * ps_ref[:, sl]
    x3 = x3_ref[...]
    h3 = _rms(x3, g2_ref[...])
    h3_ref[...] = h3
    h_hi = h3.astype(BF16)
    h_lo = (h3 - h_hi.astype(F32)).astype(BF16)
    rw = rw_ref[...]
    w_hi = rw.astype(BF16)
    w_lo = (rw - w_hi.astype(F32)).astype(BF16)
    logits = (jnp.dot(h_hi, w_hi, preferred_element_type=F32)
              + jnp.dot(h_lo, w_hi, preferred_element_type=F32)
              + jnp.dot(h_hi, w_lo, preferred_element_type=F32))
    lane = lax.broadcasted_iota(jnp.int32, logits.shape, 1)
    logits = jnp.where(lane < N_EXPERTS, logits, -jnp.inf)
    m1 = jnp.max(logits, axis=-1, keepdims=True)
    i1 = jnp.min(jnp.where(logits == m1, lane, LANES), axis=-1, keepdims=True)
    is1 = lane == i1
    rest = jnp.where(is1, -jnp.inf, logits)
    m2 = jnp.max(rest, axis=-1, keepdims=True)
    i2 = jnp.min(jnp.where(rest == m2, lane, LANES), axis=-1, keepdims=True)
    is2 = lane == i2
    e2 = jnp.exp(m2 - m1)
    p1 = 1.0 / (1.0 + e2)
    p2 = e2 / (1.0 + e2)
    gates_ref[...] = jnp.where(is1, p1, jnp.where(is2, p2, 0.0))
    sel_ref[...] = jnp.where(is1 | is2, 1.0, 0.0)


def _pool_router(x, g1, pool_w, pool_scale, g2, router_w, *, seq, tm=256):
    T, D = x.shape
    hb = tm // HALO
    full2 = lambda a: pl.BlockSpec(a.shape, lambda i: (0, 0))
    row_spec = lambda n: pl.BlockSpec((tm, n), lambda i: (i, 0))
    return pl.pallas_call(
        functools.partial(_pool_router_kernel, tm=tm, seq=seq),
        grid=(T // tm,),
        in_specs=[
            row_spec(D),
            pl.BlockSpec((HALO, D), lambda i: (jnp.maximum(i * hb - 1, 0), 0)),
            full2(g1),
            pl.BlockSpec(pool_w.shape, lambda i: (0, 0, 0)),
            full2(pool_scale), full2(g2), full2(router_w),
        ],
        out_specs=[row_spec(D), row_spec(D), row_spec(LANES), row_spec(LANES)],
        out_shape=[
            jax.ShapeDtypeStruct((T, D), F32),
            jax.ShapeDtypeStruct((T, D), F32),
            jax.ShapeDtypeStruct((T, LANES), F32),
            jax.ShapeDtypeStruct((T, LANES), F32),
        ],
        compiler_params=_cparams(("parallel",)),
        name="pool_router",
    )(x, x, g1, pool_w, pool_scale, g2, router_w)


def _rank_kernel(sel_ref, gates_ref, pos_ref, pw_ref, tile_ref, meta_ref, rank_ref, *, tb, tme):
    T = sel_ref.shape[0]
    nb = T // tb
    r = lax.broadcasted_iota(jnp.int32, (tb, tb), 0)
    c = lax.broadcasted_iota(jnp.int32, (tb, tb), 1)
    tri = jnp.where(c < r, 1.0, 0.0).astype(BF16)

    def count(b, carry):
        rows = pl.ds(pl.multiple_of(b * tb, tb), tb)
        blk = sel_ref[rows, :]
        rank_ref[rows, :] = jnp.dot(tri, blk.astype(BF16), preferred_element_type=F32) + carry
        return carry + jnp.sum(blk, axis=0, keepdims=True)

    counts = lax.fori_loop(0, nb, count, jnp.zeros((1, LANES), F32))
    padded = jnp.floor((counts + (tme - 1)) * (1.0 / tme)) * tme
    lane8 = lax.broadcasted_iota(jnp.int32, (8, LANES), 1)
    inc = jnp.broadcast_to(padded, (8, LANES))
    for sh in (1, 2, 4):
        inc = inc + jnp.where(lane8 >= sh, pltpu.roll(inc, sh, 1), 0.0)
    offs = inc[0:1, :] - padded

    def place(b, carry):
        rows = pl.ds(pl.multiple_of(b * tb, tb), tb)
        sel = sel_ref[rows, :] > 0.0
        gates = gates_ref[rows, :]
        slot = offs + rank_ref[rows, :]
        lane = lax.broadcasted_iota(jnp.int32, (tb, LANES), 1)
        first = jnp.min(jnp.where(sel, lane, LANES), axis=-1, keepdims=True)
        lo = lane == first
        hi = sel & jnp.logical_not(lo)
        pick = lambda m, a: jnp.sum(jnp.where(m, a, 0.0), axis=-1, keepdims=True)
        pos_ref[rows, :] = jnp.where(lane == 0, pick(lo, slot),
                                     jnp.where(lane == 1, pick(hi, slot), 0.0)).astype(jnp.int32)
        pw_ref[rows, :] = jnp.where(lane == 0, pick(lo, gates), jnp.where(lane == 1, pick(hi, gates), 0.0))
        return carry

    lax.fori_loop(0, nb, place, 0)

    first_slot = (lax.broadcasted_iota(jnp.int32, (LANES, LANES), 0) * tme).astype(F32)
    lane_t = lax.broadcasted_iota(jnp.int32, (LANES, LANES), 1)
    ended = jnp.where((lane_t < N_EXPERTS) & (first_slot >= inc[0:1, :]), 1.0, 0.0)
    expert = jnp.minimum(jnp.sum(ended, axis=-1, keepdims=True), N_EXPERTS - 1.0)
    total = jnp.max(inc[0:1, :], axis=-1, keepdims=True)
    group_end = jnp.sum(jnp.where(lane_t == expert.astype(jnp.int32), offs + counts, 0.0), axis=-1, keepdims=True)
    valid = jnp.clip(group_end - first_slot[:, 0:1], 0.0, float(tme))
    tile_ref[...] = jnp.where(lane_t == 0, expert, jnp.where(lane_t == 1, total * (1.0 / tme),
                              jnp.where(lane_t == 2, valid, 0.0))).astype(jnp.int32)
    row8 = lax.broadcasted_iota(jnp.int32, (8, LANES), 0)
    meta_ref[...] = jnp.where(row8 == 0, counts, jnp.where(row8 == 1, offs, jnp.where(row8 == 2, padded, 0.0))
                              ).astype(jnp.int32)


def _rank_slots(sel, gates, *, tme, tb=512):
    T = sel.shape[0]
    vm = pl.BlockSpec(memory_space=pltpu.VMEM)
    return pl.pallas_call(
        functools.partial(_rank_kernel, tb=tb, tme=tme),
        in_specs=[vm, vm],
        out_specs=[vm, vm, vm, vm],
        out_shape=[
            jax.ShapeDtypeStruct((T, LANES), jnp.int32),
            jax.ShapeDtypeStruct((T, LANES), F32),
            jax.ShapeDtypeStruct((LANES, LANES), jnp.int32),
            jax.ShapeDtypeStruct((8, LANES), jnp.int32),
        ],
        scratch_shapes=[pltpu.VMEM((T, LANES), F32)],
        compiler_params=pltpu.CompilerParams(vmem_limit_bytes=VMEM_LIMIT),
        name="rank_slots",
    )(sel, gates)


def _row_copy(src, s, dst, d, sem):
    return pltpu.make_async_copy(src.at[pl.ds(s, 1)], dst.at[pl.ds(d, 1)], sem)


def _dispatch_kernel(cnt_ref, off_ref, pad_ref, nt_ref, pos_ref, h_ref, o_hbm, zero_ref, sem, zsem, *, td, tme):
    i = pl.program_id(0)
    zrows = zero_ref.shape[0]

    def start(r, c):
        for k in range(2):
            _row_copy(h_ref, r, o_hbm, pos_ref[0, 0, 2 * r + k], sem).start()
        return c

    lax.fori_loop(0, td, start, 0, unroll=8)

    @pl.when(i == 0)
    def _():
        zero_ref[...] = jnp.zeros(zero_ref.shape, zero_ref.dtype)
        for e in range(N_EXPERTS):
            lo = off_ref[e] + cnt_ref[e]
            hi = off_ref[e] + pad_ref[e]

            def zstart(s, c):
                _row_copy(zero_ref, 0, o_hbm, s, zsem).start()
                return c

            def zwait(s, c):
                _row_copy(zero_ref, 0, o_hbm, s, zsem).wait()
                return c

            lax.fori_loop(lo, hi, zstart, 0)
            lax.fori_loop(lo, hi, zwait, 0)

        def chunk(j):
            return pltpu.make_async_copy(zero_ref, o_hbm.at[pl.ds(pl.multiple_of(j * zrows, zrows), zrows)], zsem)

        c_lo = nt_ref[0] * (tme // zrows)
        c_hi = o_hbm.shape[0] // zrows
        lax.fori_loop(c_lo, c_hi, lambda j, c: (chunk(j).start(), c)[1], 0)
        lax.fori_loop(c_lo, c_hi, lambda j, c: (chunk(j).wait(), c)[1], 0)

    def wait(r, c):
        for k in range(2):
            _row_copy(h_ref, r, o_hbm, pos_ref[0, 0, 2 * r + k], sem).wait()
        return c

    lax.fori_loop(0, td, wait, 0, unroll=8)


def _dispatch(h, pos2, counts, offs, padded, n_tiles, *, n_slots, tme, td=256, zrows=256):
    T, D = h.shape
    nblk = T // td
    pos_blk = pos2.reshape(nblk, 1, 2 * td)
    grid_spec = pltpu.PrefetchScalarGridSpec(
        num_scalar_prefetch=4,
        grid=(nblk,),
        in_specs=[
            pl.BlockSpec((1, 1, 2 * td), lambda i, *_: (i, 0, 0), memory_space=pltpu.SMEM),
            pl.BlockSpec((td, D), lambda i, *_: (i, 0)),
        ],
        out_specs=pl.BlockSpec(memory_space=pl.ANY),
        scratch_shapes=[
            pltpu.VMEM((zrows, D), h.dtype),
            pltpu.SemaphoreType.DMA(()),
            pltpu.SemaphoreType.DMA(()),
        ],
    )
    return pl.pallas_call(
        functools.partial(_dispatch_kernel, td=td, tme=tme),
        grid_spec=grid_spec,
        out_shape=jax.ShapeDtypeStruct((n_slots, D), h.dtype),
        compiler_params=pltpu.CompilerParams(dimension_semantics=("arbitrary",), vmem_limit_bytes=VMEM_LIMIT),
        name="dispatch",
    )(counts, offs, padded, n_tiles, pos_blk, h)


def _experts_kernel(te_ref, nt_ref, tv_ref, x_ref, wg_ref, wu_ref, wd_ref, o_ref, xb_ref):
    i = pl.program_id(0)
    k = pl.program_id(1)
    half = x_ref.shape[0] // 2

    def swiglu_rows(n):
        xb = xb_ref[:n, :]
        cast = lambda w_ref: w_ref[0].astype(xb.dtype)
        g = jnp.dot(xb, cast(wg_ref), preferred_element_type=F32)
        u = jnp.dot(xb, cast(wu_ref), preferred_element_type=F32)
        a = (g * jax.nn.sigmoid(g) * u).astype(xb.dtype)
        o_ref[:n, :] += jnp.dot(a, cast(wd_ref), preferred_element_type=F32)

    @pl.when(i < nt_ref[0])
    def _():
        @pl.when(k == 0)
        def _():
            xb_ref[...] = x_ref[...].astype(xb_ref.dtype)
            o_ref[...] = jnp.zeros(o_ref.shape, o_ref.dtype)

        @pl.when(tv_ref[i] > half)
        def _():
            swiglu_rows(2 * half)

        @pl.when(tv_ref[i] <= half)
        def _():
            swiglu_rows(half)

    @pl.when((i >= nt_ref[0]) & (k == 0))
    def _():
        o_ref[...] = jnp.zeros(o_ref.shape, o_ref.dtype)


def _experts(xs, tile_expert, n_tiles, tile_valid, wg, wu, wd, *, tme, tf=256):
    NS, D = xs.shape
    nk = wg.shape[2] // tf
    nt = NS // tme

    def row_map(i, k, te, n, tv):
        return (jnp.minimum(i, n[0] - 1), 0)

    def chunk(i, k, n):
        return jnp.where(i < n[0], k, nk - 1)

    grid_spec = pltpu.PrefetchScalarGridSpec(
        num_scalar_prefetch=3,
        grid=(nt, nk),
        in_specs=[
            pl.BlockSpec((tme, D), row_map),
            pl.BlockSpec((1, D, tf), lambda i, k, te, n, tv: (te[i], 0, chunk(i, k, n))),
            pl.BlockSpec((1, D, tf), lambda i, k, te, n, tv: (te[i], 0, chunk(i, k, n))),
            pl.BlockSpec((1, tf, D), lambda i, k, te, n, tv: (te[i], chunk(i, k, n), 0)),
        ],
        out_specs=pl.BlockSpec((tme, D), lambda i, k, te, n, tv: (i, 0)),
        scratch_shapes=[pltpu.VMEM((tme, D), BF16)],
    )
    return pl.pallas_call(
        _experts_kernel,
        grid_spec=grid_spec,
        out_shape=jax.ShapeDtypeStruct((NS, D), F32),
        compiler_params=_cparams(("arbitrary", "arbitrary")),
        name="experts",
    )(tile_expert, n_tiles, tile_valid, xs, wg, wu, wd)


def _combine_kernel(pos_ref, nxt_ref, x_ref, pw_ref, g_ref, y_hbm, o_ref, lo_ref, hi_ref, sem, *, tc):
    i = pl.program_id(0)
    slot = i % 2

    def copies(p_ref, s, r):
        return (_row_copy(y_hbm, p_ref[0, 0, 2 * r], lo_ref.at[s], r, sem.at[s]),
                _row_copy(y_hbm, p_ref[0, 0, 2 * r + 1], hi_ref.at[s], r, sem.at[s]))

    def issue(p_ref, s):
        def body(r, c):
            for cp in copies(p_ref, s, r):
                cp.start()
            return c
        lax.fori_loop(0, tc, body, 0, unroll=8)

    @pl.when(i == 0)
    def _():
        issue(pos_ref, 0)

    @pl.when(i + 1 < pl.num_programs(0))
    def _():
        issue(nxt_ref, 1 - slot)

    def wait(r, c):
        for cp in copies(pos_ref, slot, r):
            cp.wait()
        return c

    lax.fori_loop(0, tc, wait, 0, unroll=8)
    pw = pw_ref[...]
    y = x_ref[...] + pw[:, 0:1] * lo_ref[slot] + pw[:, 1:2] * hi_ref[slot]
    o_ref[...] = _rms(y, g_ref[...])


def _combine(x, pw, g, ys, pos2, *, tc=256):
    T, D = x.shape
    nblk = T // tc
    pos_blk = pos2.reshape(nblk, 1, 2 * tc)
    return pl.pallas_call(
        functools.partial(_combine_kernel, tc=tc),
        grid=(nblk,),
        in_specs=[
            pl.BlockSpec((1, 1, 2 * tc), lambda i: (i, 0, 0), memory_space=pltpu.SMEM),
            pl.BlockSpec((1, 1, 2 * tc), lambda i: (jnp.minimum(i + 1, nblk - 1), 0, 0), memory_space=pltpu.SMEM),
            pl.BlockSpec((tc, D), lambda i: (i, 0)),
            pl.BlockSpec((tc, LANES), lambda i: (i, 0)),
            pl.BlockSpec((1, D), lambda i: (0, 0)),
            pl.BlockSpec(memory_space=pl.ANY),
        ],
        out_specs=pl.BlockSpec((tc, D), lambda i: (i, 0)),
        out_shape=jax.ShapeDtypeStruct((T, D), F32),
        scratch_shapes=[
            pltpu.VMEM((2, tc, D), F32),
            pltpu.VMEM((2, tc, D), F32),
            pltpu.SemaphoreType.DMA((2,)),
        ],
        compiler_params=_cparams(("arbitrary",)),
        name="combine",
    )(pos_blk, pos_blk, x, pw, g, ys)


def _rope_tables(seq):
    pos = jnp.arange(seq, dtype=F32)
    inv_freq = ROPE_THETA ** (-jnp.arange(0, QK_ROPE, 2, dtype=F32) / QK_ROPE)
    ang = pos[:, None] * inv_freq[None, :]
    cos, sin = jnp.cos(ang), jnp.sin(ang)
    z32 = jnp.zeros_like(cos)
    z64 = jnp.zeros((seq, LANES - QK_ROPE), F32)
    cos_f = jnp.concatenate([cos, cos, z64], axis=1)
    sin_a = jnp.concatenate([-sin, z32, z64], axis=1)
    sin_b = jnp.concatenate([z32, sin, z64], axis=1)
    return cos_f, sin_a, sin_b


def kernel(x, norm_mix0, w_in, conv_w, q_norm, w_uq, kv_norm, w_ukv, w_out, norm_ffn0, ffn_w_gate,
           ffn_w_up, ffn_w_down, norm_mix1, pool_w, pool_scale, norm_ffn1, router_w, moe_w_gate,
           moe_w_up, moe_w_down, final_norm):
    B, S, D = x.shape
    T = B * S
    H = MLA_HEADS
    tme = 1024
    n_slots = 2 * T + N_EXPERTS * tme

    xf = x.reshape(T, D)

    w_in_p = jnp.pad(w_in[0], ((0, 0), (0, IN_COLS_PAD - w_in.shape[2]))).astype(BF16)
    wq = w_uq[0].reshape(Q_LORA, H, QK_NOPE + QK_ROPE)
    wq = jnp.pad(wq, ((0, 0), (0, 0), (0, HEAD_PAD - QK_NOPE - QK_ROPE))).reshape(Q_LORA, H * HEAD_PAD).astype(BF16)
    wkv = w_ukv[0].reshape(KV_LORA, H, QK_NOPE + V_HEAD)
    wk = wkv[:, :, :QK_NOPE].reshape(KV_LORA, H * QK_NOPE).astype(BF16)
    wv = wkv[:, :, QK_NOPE:].reshape(KV_LORA, H * V_HEAD).astype(BF16)
    wo_a = w_out[0, :CONV_DIM].astype(BF16)
    wo_b = w_out[0, CONV_DIM:].astype(BF16)
    rw = jnp.pad(router_w[0], ((0, 0), (0, LANES - N_EXPERTS)))
    cos_f, sin_a, sin_b = _rope_tables(S)

    proj = _norm_inproj(xf, norm_mix0, w_in_p)
    y_a = _gated_conv(proj, conv_w[0], seq=S)
    q, k, v = _qkv_proj(proj, q_norm, kv_norm, wq, wk, wv, cos_f, sin_a, sin_b, seq=S)
    y_b = _flash_attention(q, k, v, batch=B, seq=S)
    x1 = _out_proj(y_a, y_b, wo_a, wo_b, xf)
    x2 = _dense_ffn(x1, norm_ffn0, ffn_w_gate[0].astype(BF16), ffn_w_up[0].astype(BF16), ffn_w_down[0].astype(BF16))

    x3, h3, gates, sel = _pool_router(x2, norm_mix1, pool_w[0].astype(BF16), pool_scale, norm_ffn1, rw, seq=S)
    pos, pw, tiles, meta = _rank_slots(sel, gates, tme=tme)
    pos2 = pos[:, :2]
    nt = n_slots // tme
    n_tiles = tiles[0:1, 1]
    xs = _dispatch(h3, pos2, meta[0, :N_EXPERTS], meta[1, :N_EXPERTS], meta[2, :N_EXPERTS], n_tiles,
                   n_slots=n_slots, tme=tme)
    ys = _experts(xs, tiles[:nt, 0], n_tiles, tiles[:nt, 2], moe_w_gate[0], moe_w_up[0], moe_w_down[0], tme=tme)
    out = _combine(x3, pw, final_norm.reshape(1, D), ys, pos2)
    return out.reshape(B, S, D)
```

```python
import functools

import jax
import jax.numpy as jnp
from jax import lax
from jax.experimental import pallas as pl
from jax.experimental.pallas import tpu as pltpu

F32 = jnp.float32
BF16 = jnp.bfloat16

EPS = 1e-6
CONV_DIM = 1024
MLA_HEADS = 8
Q_LORA = 512
KV_LORA = 256
QK_NOPE = 128
QK_ROPE = 64
V_HEAD = 128
ROPE_THETA = 10000.0
POOL_WINDOWS = (2, 4, 8, 16)
N_EXPERTS = 8
LOG2E = 1.4426950408889634

LANES = 128
HEAD_PAD = 256
IN_COLS_PAD = 4096
LAT_OFF = 3 * CONV_DIM
HALO = 16
VMEM_LIMIT = 56 * 1024 * 1024


def _cparams(sem):
    return pltpu.CompilerParams(dimension_semantics=sem, vmem_limit_bytes=VMEM_LIMIT)


def _rms(x, g):
    return x * lax.rsqrt(jnp.mean(x * x, axis=-1, keepdims=True) + EPS) * g


def _norm_inproj_kernel(x_ref, g_ref, w_ref, o_ref, h_ref):
    @pl.when(pl.program_id(1) == 0)
    def _():
        h_ref[...] = _rms(x_ref[...], g_ref[...]).astype(BF16)

    o_ref[...] = jnp.dot(h_ref[...], w_ref[...], preferred_element_type=F32).astype(o_ref.dtype)


def _norm_inproj(x, g, w, *, tm=1024, tn=1024):
    T, D = x.shape
    N = w.shape[1]
    return pl.pallas_call(
        _norm_inproj_kernel,
        grid=(T // tm, N // tn),
        in_specs=[
            pl.BlockSpec((tm, D), lambda i, j: (i, 0)),
            pl.BlockSpec((1, D), lambda i, j: (0, 0)),
            pl.BlockSpec((D, tn), lambda i, j: (0, j)),
        ],
        out_specs=pl.BlockSpec((tm, tn), lambda i, j: (i, j)),
        out_shape=jax.ShapeDtypeStruct((T, N), BF16),
        scratch_shapes=[pltpu.VMEM((tm, D), BF16)],
        compiler_params=_cparams(("parallel", "arbitrary")),
        name="norm_inproj",
    )(x, g, w)


def _conv_kernel(xin_ref, gb_ref, gc_ref, hx_ref, hc_ref, w_ref, o_ref, *, ts, seq):
    i = pl.program_id(0)
    u = gc_ref[...].astype(F32) * xin_ref[...].astype(F32)
    hu = hc_ref[...].astype(F32) * hx_ref[...].astype(F32)
    hu = jnp.where((i * ts) % seq == 0, 0.0, hu)
    hm1 = hu[HALO - 1:HALO, :]
    hm2 = hu[HALO - 2:HALO - 1, :]
    row = lax.broadcasted_iota(jnp.int32, u.shape, 0)
    u1 = jnp.where(row == 0, hm1, pltpu.roll(u, 1, 0))
    u2 = jnp.where(row == 0, hm2, jnp.where(row == 1, hm1, pltpu.roll(u, 2, 0)))
    w = w_ref[...]
    y = w[0:1, :] * u2 + w[1:2, :] * u1 + w[2:3, :] * u
    o_ref[...] = (gb_ref[...].astype(F32) * y).astype(o_ref.dtype)


def _gated_conv(proj, conv_w, *, seq, ts=512):
    T = proj.shape[0]
    C = CONV_DIM
    hb = ts // HALO
    halo_map = lambda c: (lambda i: (jnp.maximum(i * hb - 1, 0), c))
    return pl.pallas_call(
        functools.partial(_conv_kernel, ts=ts, seq=seq),
        grid=(T // ts,),
        in_specs=[
            pl.BlockSpec((ts, C), lambda i: (i, 0)),
            pl.BlockSpec((ts, C), lambda i: (i, 1)),
            pl.BlockSpec((ts, C), lambda i: (i, 2)),
            pl.BlockSpec((HALO, C), halo_map(0)),
            pl.BlockSpec((HALO, C), halo_map(2)),
            pl.BlockSpec(conv_w.shape, lambda i: (0, 0)),
        ],
        out_specs=pl.BlockSpec((ts, C), lambda i: (i, 0)),
        out_shape=jax.ShapeDtypeStruct((T, C), BF16),
        compiler_params=_cparams(("parallel",)),
        name="gated_conv",
    )(proj, proj, proj, proj, proj, conv_w)


def _rope128(blk, cos_f, sin_a, sin_b):
    return blk * cos_f + pltpu.roll(blk, 96, 1) * sin_a + pltpu.roll(blk, 32, 1) * sin_b


def _qkv_kernel(lat_ref, qn_ref, kvn_ref, wq_ref, wk_ref, wv_ref, cos_ref, sa_ref, sb_ref,
                q_ref, k_ref, v_ref, *, scale):
    lat = lat_ref[...].astype(F32)
    cq = _rms(lat[:, :Q_LORA], qn_ref[...]).astype(BF16)
    ckv = _rms(lat[:, Q_LORA:Q_LORA + KV_LORA], kvn_ref[...]).astype(BF16)
    kr = lat[:, Q_LORA + KV_LORA:Q_LORA + KV_LORA + LANES]
    cos_f, sin_a, sin_b = cos_ref[...], sa_ref[...], sb_ref[...]
    q = jnp.dot(cq, wq_ref[...], preferred_element_type=F32) * scale
    kn = jnp.dot(ckv, wk_ref[...], preferred_element_type=F32)
    vv = jnp.dot(ckv, wv_ref[...], preferred_element_type=F32)
    kr_rot = _rope128(kr, cos_f, sin_a, sin_b).astype(k_ref.dtype)
    ones = jnp.ones((lat.shape[0], V_HEAD), v_ref.dtype)
    for h in range(MLA_HEADS):
        o = h * HEAD_PAD
        v_ref[:, o:o + V_HEAD] = vv[:, h * V_HEAD:(h + 1) * V_HEAD].astype(v_ref.dtype)
        v_ref[:, o + V_HEAD:o + HEAD_PAD] = ones
        q_ref[:, o:o + QK_NOPE] = q[:, o:o + QK_NOPE].astype(q_ref.dtype)
        q_ref[:, o + QK_NOPE:o + HEAD_PAD] = _rope128(
            q[:, o + QK_NOPE:o + HEAD_PAD], cos_f, sin_a, sin_b).astype(q_ref.dtype)
        k_ref[:, o:o + QK_NOPE] = kn[:, h * QK_NOPE:(h + 1) * QK_NOPE].astype(k_ref.dtype)
        k_ref[:, o + QK_NOPE:o + HEAD_PAD] = kr_rot


def _qkv_proj(proj, q_norm, kv_norm, wq, wk, wv, cos_f, sin_a, sin_b, *, seq, tm=512):
    T = proj.shape[0]
    H = MLA_HEADS
    lat_blk = LAT_OFF // 1024
    nseq = seq // tm
    tab = pl.BlockSpec((tm, LANES), lambda i: (i % nseq, 0))
    full = lambda a: pl.BlockSpec(a.shape, lambda i: (0, 0))
    return pl.pallas_call(
        functools.partial(_qkv_kernel, scale=float((QK_NOPE + QK_ROPE) ** -0.5 * LOG2E)),
        grid=(T // tm,),
        in_specs=[
            pl.BlockSpec((tm, 1024), lambda i: (i, lat_blk)),
            full(q_norm), full(kv_norm), full(wq), full(wk), full(wv),
            tab, tab, tab,
        ],
        out_specs=[
            pl.BlockSpec((tm, H * HEAD_PAD), lambda i: (i, 0)),
            pl.BlockSpec((tm, H * HEAD_PAD), lambda i: (i, 0)),
            pl.BlockSpec((tm, H * HEAD_PAD), lambda i: (i, 0)),
        ],
        out_shape=[jax.ShapeDtypeStruct((T, H * HEAD_PAD), BF16)] * 3,
        compiler_params=_cparams(("parallel",)),
        name="qkv_proj",
    )(proj, q_norm, kv_norm, wq, wk, wv, cos_f, sin_a, sin_b)


def _flash_kernel(q_ref, k_ref, v_ref, o_ref, m_ref, acc_ref, *, tq, hp):
    qi = pl.program_id(2)
    m_ref[...] = jnp.full(m_ref.shape, -jnp.inf, F32)
    acc_ref[...] = jnp.zeros(acc_ref.shape, F32)
    nc = tq // LANES

    def block(j, masked):
        start = pl.multiple_of(j * tq, tq)
        for h in range(hp):
            hs = slice(h * HEAD_PAD, (h + 1) * HEAD_PAD)
            q = q_ref[:, hs]
            k = k_ref[pl.ds(start, tq), hs]
            v = v_ref[pl.ds(start, tq), hs]
            s = lax.dot_general(q, k, (((1,), (1,)), ((), ())), preferred_element_type=F32)
            if masked:
                row = lax.broadcasted_iota(jnp.int32, s.shape, 0)
                col = lax.broadcasted_iota(jnp.int32, s.shape, 1)
                s = jnp.where(col <= row, s, -jnp.inf)
            cols = [s[:, c * LANES:(c + 1) * LANES] for c in range(nc)]
            mx = functools.reduce(jnp.maximum, cols)
            m_prev = m_ref[h]
            m_new = jnp.maximum(m_prev, jnp.broadcast_to(jnp.max(mx, axis=-1, keepdims=True), m_prev.shape))
            alpha = jnp.exp2(m_prev - m_new)
            p = jnp.concatenate([jnp.exp2(c - m_new) for c in cols], axis=1).astype(v.dtype)
            pv = jnp.dot(p, v, preferred_element_type=F32)
            acc_ref[h] = jnp.concatenate([alpha, alpha], axis=1) * acc_ref[h] + pv
            m_ref[h] = m_new

    def body(j, c):
        block(j, False)
        return c

    lax.fori_loop(0, qi, body, 0)
    block(qi, True)
    for h in range(hp):
        acc = acc_ref[h]
        o_ref[:, h * V_HEAD:(h + 1) * V_HEAD] = (acc[:, :V_HEAD] / acc[:, V_HEAD:]).astype(o_ref.dtype)


def _flash_attention(q, k, v, *, batch, seq, tq=512, hp=8):
    T = q.shape[0]
    H = MLA_HEADS
    nq = seq // tq
    return pl.pallas_call(
        functools.partial(_flash_kernel, tq=tq, hp=hp),
        grid=(batch, H // hp, nq),
        in_specs=[
            pl.BlockSpec((tq, hp * HEAD_PAD), lambda b, h, i: (b * nq + i, h)),
            pl.BlockSpec((seq, hp * HEAD_PAD), lambda b, h, i: (b, h), pipeline_mode=pl.Buffered(1)),
            pl.BlockSpec((seq, hp * HEAD_PAD), lambda b, h, i: (b, h), pipeline_mode=pl.Buffered(1)),
        ],
        out_specs=pl.BlockSpec((tq, hp * V_HEAD), lambda b, h, i: (b * nq + i, h)),
        out_shape=jax.ShapeDtypeStruct((T, H * V_HEAD), BF16),
        scratch_shapes=[
            pltpu.VMEM((hp, tq, LANES), F32),
            pltpu.VMEM((hp, tq, HEAD_PAD), F32),
        ],
        compiler_params=_cparams(("parallel", "parallel", "arbitrary")),
        name="flash_attention",
    )(q, k, v)


def _outproj_kernel(ya_ref, yb_ref, wa_ref, wb_ref, x_ref, x1_ref):
    acc = jnp.dot(ya_ref[...], wa_ref[...], preferred_element_type=F32)
    acc = acc + jnp.dot(yb_ref[...], wb_ref[...], preferred_element_type=F32)
    x1_ref[...] = x_ref[...] + acc


def _out_proj(ya, yb, wa, wb, x, *, tm=512):
    T, D = x.shape
    full = lambda a: pl.BlockSpec(a.shape, lambda i: (0, 0))
    return pl.pallas_call(
        _outproj_kernel,
        grid=(T // tm,),
        in_specs=[
            pl.BlockSpec((tm, ya.shape[1]), lambda i: (i, 0)),
            pl.BlockSpec((tm, yb.shape[1]), lambda i: (i, 0)),
            full(wa), full(wb),
            pl.BlockSpec((tm, D), lambda i: (i, 0)),
        ],
        out_specs=pl.BlockSpec((tm, D), lambda i: (i, 0)),
        out_shape=jax.ShapeDtypeStruct((T, D), F32),
        compiler_params=_cparams(("parallel",)),
        name="out_proj",
    )(ya, yb, wa, wb, x)


def _ffn_kernel(x_ref, g_ref, wg_ref, wu_ref, wd_ref, o_ref, h_ref):
    k = pl.program_id(1)

    @pl.when(k == 0)
    def _():
        x = x_ref[...]
        h_ref[...] = _rms(x, g_ref[...]).astype(h_ref.dtype)
        o_ref[...] = x

    h = h_ref[...]
    g = jnp.dot(h, wg_ref[...], preferred_element_type=F32)
    u = jnp.dot(h, wu_ref[...], preferred_element_type=F32)
    a = (g * jax.nn.sigmoid(g) * u).astype(h.dtype)
    o_ref[...] += jnp.dot(a, wd_ref[...], preferred_element_type=F32)


def _dense_ffn(x, g, wg, wu, wd, *, tm=1024, tf=512):
    T, D = x.shape
    F = wg.shape[1]
    return pl.pallas_call(
        _ffn_kernel,
        grid=(T // tm, F // tf),
        in_specs=[
            pl.BlockSpec((tm, D), lambda i, k: (i, 0)),
            pl.BlockSpec((1, D), lambda i, k: (0, 0)),
            pl.BlockSpec((D, tf), lambda i, k: (0, k)),
            pl.BlockSpec((D, tf), lambda i, k: (0, k)),
            pl.BlockSpec((tf, D), lambda i, k: (k, 0)),
        ],
        out_specs=pl.BlockSpec((tm, D), lambda i, k: (i, 0)),
        out_shape=jax.ShapeDtypeStruct((T, D), F32),
        scratch_shapes=[pltpu.VMEM((tm, D), BF16)],
        compiler_params=_cparams(("parallel", "arbitrary")),
        name="dense_ffn",
    )(x, g, wg, wu, wd)


def _pool_router_kernel(x_ref, halo_ref, g1_ref, pw_ref, ps_ref, g2_ref, rw_ref,
                        x3_ref, h3_ref, gates_ref, sel_ref, *, tm, seq):
    i = pl.program_id(0)
    x = x_ref[...]
    g1 = g1_ref[...]
    h = _rms(x, g1)
    hh = _rms(halo_ref[...], g1)
    t0 = (i * tm) % seq
    hh = jnp.where(t0 == 0, 0.0, hh)
    tpos = t0 + lax.broadcasted_iota(jnp.int32, (tm, 1), 0)
    pd = x.shape[1] // len(POOL_WINDOWS)
    for g, w in enumerate(POOL_WINDOWS):
        sl = slice(g * pd, (g + 1) * pd)
        hg = h[:, sl]
        cur = jnp.concatenate([hh[:, sl], hg], axis=0)
        sh = 1
        while sh < w:
            cur = cur + pltpu.roll(cur, sh, 0)
            sh *= 2
        wsum = cur[HALO:, :]
        inv_cnt = 1.0 / jnp.minimum(tpos + 1, w).astype(F32)
        pooled = (wsum * inv_cnt - hg).astype(BF16)
        yg = jnp.dot(pooled, pw_ref[g], preferred_element_type=F32)
        x3_ref[:, sl] = x[:, sl] + yg * ps_ref[:, sl]
    x3 = x3_ref[...]
    h3 = _rms(x3, g2_ref[...])
    h3_ref[...] = h3
    h_hi = h3.astype(BF16)
    h_lo = (h3 - h_hi.astype(F32)).astype(BF16)
    rw = rw_ref[...]
    w_hi = rw.astype(BF16)
    w_lo = (rw - w_hi.astype(F32)).astype(BF16)
    logits = (jnp.dot(h_hi, w_hi, preferred_element_type=F32)
              + jnp.dot(h_lo, w_hi, preferred_element_type=F32)
              + jnp.dot(h_hi, w_lo, preferred_element_type=F32))
    lane = lax.broadcasted_iota(jnp.int32, logits.shape, 1)
    logits = jnp.where(lane < N_EXPERTS, logits, -jnp.inf)
    m1 = jnp.max(logits, axis=-1, keepdims=True)
    i1 = jnp.min(jnp.where(logits == m1, lane, LANES), axis=-1, keepdims=True)
    is1 = lane == i1
    rest = jnp.where(is1, -jnp.inf, logits)
    m2 = jnp.max(rest, axis=-1, keepdims=True)
    i2 = jnp.min(jnp.where(rest == m2, lane, LANES), axis=-1, keepdims=True)
    is2 = lane == i2
    e2 = jnp.exp(m2 - m1)
    p1 = 1.0 / (1.0 + e2)
    p2 = e2 / (1.0 + e2)
    gates_ref[...] = jnp.where(is1, p1, jnp.where(is2, p2, 0.0))
    sel_ref[...] = jnp.where(is1 | is2, 1.0, 0.0)


def _pool_router(x, g1, pool_w, pool_scale, g2, router_w, *, seq, tm=256):
    T, D = x.shape
    hb = tm // HALO
    full2 = lambda a: pl.BlockSpec(a.shape, lambda i: (0, 0))
    row_spec = lambda n: pl.BlockSpec((tm, n), lambda i: (i, 0))
    return pl.pallas_call(
        functools.partial(_pool_router_kernel, tm=tm, seq=seq),
        grid=(T // tm,),
        in_specs=[
            row_spec(D),
            pl.BlockSpec((HALO, D), lambda i: (jnp.maximum(i * hb - 1, 0), 0)),
            full2(g1),
            pl.BlockSpec(pool_w.shape, lambda i: (0, 0, 0)),
            full2(pool_scale), full2(g2), full2(router_w),
        ],
        out_specs=[row_spec(D), row_spec(D), row_spec(LANES), row_spec(LANES)],
        out_shape=[
            jax.ShapeDtypeStruct((T, D), F32),
            jax.ShapeDtypeStruct((T, D), F32),
            jax.ShapeDtypeStruct((T, LANES), F32),
            jax.ShapeDtypeStruct((T, LANES), F32),
        ],
        compiler_params=_cparams(("parallel",)),
        name="pool_router",
    )(x, x, g1, pool_w, pool_scale, g2, router_w)


def _rank_kernel(sel_ref, gates_ref, pos_ref, pw_ref, tile_ref, meta_ref, rank_ref, *, tb, tme):
    T = sel_ref.shape[0]
    nb = T // tb
    r = lax.broadcasted_iota(jnp.int32, (tb, tb), 0)
    c = lax.broadcasted_iota(jnp.int32, (tb, tb), 1)
    tri = jnp.where(c < r, 1.0, 0.0).astype(BF16)

    def count(b, carry):
        rows = pl.ds(pl.multiple_of(b * tb, tb), tb)
        blk = sel_ref[rows, :]
        rank_ref[rows, :] = jnp.dot(tri, blk.astype(BF16), preferred_element_type=F32) + carry
        return carry + jnp.sum(blk, axis=0, keepdims=True)

    counts = lax.fori_loop(0, nb, count, jnp.zeros((1, LANES), F32))
    padded = jnp.floor((counts + (tme - 1)) * (1.0 / tme)) * tme
    lane8 = lax.broadcasted_iota(jnp.int32, (8, LANES), 1)
    inc = jnp.broadcast_to(padded, (8, LANES))
    for sh in (1, 2, 4):
        inc = inc + jnp.where(lane8 >= sh, pltpu.roll(inc, sh, 1), 0.0)
    offs = inc[0:1, :] - padded

    def place(b, carry):
        rows = pl.ds(pl.multiple_of(b * tb, tb), tb)
        sel = sel_ref[rows, :] > 0.0
        gates = gates_ref[rows, :]
        slot = offs + rank_ref[rows, :]
        lane = lax.broadcasted_iota(jnp.int32, (tb, LANES), 1)
        first = jnp.min(jnp.where(sel, lane, LANES), axis=-1, keepdims=True)
        lo = lane == first
        hi = sel & jnp.logical_not(lo)
        pick = lambda m, a: jnp.sum(jnp.where(m, a, 0.0), axis=-1, keepdims=True)
        pos_ref[rows, :] = jnp.where(lane == 0, pick(lo, slot),
                                     jnp.where(lane == 1, pick(hi, slot), 0.0)).astype(jnp.int32)
        pw_ref[rows, :] = jnp.where(lane == 0, pick(lo, gates), jnp.where(lane == 1, pick(hi, gates), 0.0))
        return carry

    lax.fori_loop(0, nb, place, 0)

    first_slot = (lax.broadcasted_iota(jnp.int32, (LANES, LANES), 0) * tme).astype(F32)
    lane_t = lax.broadcasted_iota(jnp.int32, (LANES, LANES), 1)
    ended = jnp.where((lane_t < N_EXPERTS) & (first_slot >= inc[0:1, :]), 1.0, 0.0)
    expert = jnp.minimum(jnp.sum(ended, axis=-1, keepdims=True), N_EXPERTS - 1.0)
    total = jnp.max(inc[0:1, :], axis=-1, keepdims=True)
    group_end = jnp.sum(jnp.where(lane_t == expert.astype(jnp.int32), offs + counts, 0.0), axis=-1, keepdims=True)
    valid = jnp.clip(group_end - first_slot[:, 0:1], 0.0, float(tme))
    tile_ref[...] = jnp.where(lane_t == 0, expert, jnp.where(lane_t == 1, total * (1.0 / tme),
                              jnp.where(lane_t == 2, valid, 0.0))).astype(jnp.int32)
    row8 = lax.broadcasted_iota(jnp.int32, (8, LANES), 0)
    meta_ref[...] = jnp.where(row8 == 0, counts, jnp.where(row8 == 1, offs, jnp.where(row8 == 2, padded, 0.0))
                              ).astype(jnp.int32)


def _rank_slots(sel, gates, *, tme, tb=512):
    T = sel.shape[0]
    vm = pl.BlockSpec(memory_space=pltpu.VMEM)
    return pl.pallas_call(
        functools.partial(_rank_kernel, tb=tb, tme=tme),
        in_specs=[vm, vm],
        out_specs=[vm, vm, vm, vm],
        out_shape=[
            jax.ShapeDtypeStruct((T, LANES), jnp.int32),
            jax.ShapeDtypeStruct((T, LANES), F32),
            jax.ShapeDtypeStruct((LANES, LANES), jnp.int32),
            jax.ShapeDtypeStruct((8, LANES), jnp.int32),
        ],
        scratch_shapes=[pltpu.VMEM((T, LANES), F32)],
        compiler_params=pltpu.CompilerParams(vmem_limit_bytes=VMEM_LIMIT),
        name="rank_slots",
    )(sel, gates)


def _row_copy(src, s, dst, d, sem):
    return pltpu.make_async_copy(src.at[pl.ds(s, 1)], dst.at[pl.ds(d, 1)], sem)


def _dispatch_kernel(cnt_ref, off_ref, pad_ref, nt_ref, pos_ref, h_ref, o_hbm, zero_ref, sem, zsem, *, td, tme):
    i = pl.program_id(0)
    zrows = zero_ref.shape[0]

    def start(r, c):
        for k in range(2):
            _row_copy(h_ref, r, o_hbm, pos_ref[0, 0, 2 * r + k], sem).start()
        return c

    lax.fori_loop(0, td, start, 0, unroll=8)

    @pl.when(i == 0)
    def _():
        zero_ref[...] = jnp.zeros(zero_ref.shape, zero_ref.dtype)
        for e in range(N_EXPERTS):
            lo = off_ref[e] + cnt_ref[e]
            hi = off_ref[e] + pad_ref[e]

            def zstart(s, c):
                _row_copy(zero_ref, 0, o_hbm, s, zsem).start()
                return c

            def zwait(s, c):
                _row_copy(zero_ref, 0, o_hbm, s, zsem).wait()
                return c

            lax.fori_loop(lo, hi, zstart, 0)
            lax.fori_loop(lo, hi, zwait, 0)

        def chunk(j):
            return pltpu.make_async_copy(zero_ref, o_hbm.at[pl.ds(pl.multiple_of(j * zrows, zrows), zrows)], zsem)

        c_lo = nt_ref[0] * (tme // zrows)
        c_hi = o_hbm.shape[0] // zrows
        lax.fori_loop(c_lo, c_hi, lambda j, c: (chunk(j).start(), c)[1], 0)
        lax.fori_loop(c_lo, c_hi, lambda j, c: (chunk(j).wait(), c)[1], 0)

    def wait(r, c):
        for k in range(2):
            _row_copy(h_ref, r, o_hbm, pos_ref[0, 0, 2 * r + k], sem).wait()
        return c

    lax.fori_loop(0, td, wait, 0, unroll=8)


def _dispatch(h, pos2, counts, offs, padded, n_tiles, *, n_slots, tme, td=256, zrows=256):
    T, D = h.shape
    nblk = T // td
    pos_blk = pos2.reshape(nblk, 1, 2 * td)
    grid_spec = pltpu.PrefetchScalarGridSpec(
        num_scalar_prefetch=4,
        grid=(nblk,),
        in_specs=[
            pl.BlockSpec((1, 1, 2 * td), lambda i, *_: (i, 0, 0), memory_space=pltpu.SMEM),
            pl.BlockSpec((td, D), lambda i, *_: (i, 0)),
        ],
        out_specs=pl.BlockSpec(memory_space=pl.ANY),
        scratch_shapes=[
            pltpu.VMEM((zrows, D), h.dtype),
            pltpu.SemaphoreType.DMA(()),
            pltpu.SemaphoreType.DMA(()),
        ],
    )
    return pl.pallas_call(
        functools.partial(_dispatch_kernel, td=td, tme=tme),
        grid_spec=grid_spec,
        out_shape=jax.ShapeDtypeStruct((n_slots, D), h.dtype),
        compiler_params=pltpu.CompilerParams(dimension_semantics=("arbitrary",), vmem_limit_bytes=VMEM_LIMIT),
        name="dispatch",
    )(counts, offs, padded, n_tiles, pos_blk, h)


def _experts_kernel(te_ref, nt_ref, tv_ref, x_ref, wg_ref, wu_ref, wd_ref, o_ref, xb_ref):
    i = pl.program_id(0)
    k = pl.program_id(1)
    half = x_ref.shape[0] // 2

    def swiglu_rows(n):
        xb = xb_ref[:n, :]
        cast = lambda w_ref: w_ref[0].astype(xb.dtype)
        g = jnp.dot(xb, cast(wg_ref), preferred_element_type=F32)
        u = jnp.dot(xb, cast(wu_ref), preferred_element_type=F32)
        a = (g * jax.nn.sigmoid(g) * u).astype(xb.dtype)
        o_ref[:n, :] += jnp.dot(a, cast(wd_ref), preferred_element_type=F32)

    @pl.when(i < nt_ref[0])
    def _():
        @pl.when(k == 0)
        def _():
            xb_ref[...] = x_ref[...].astype(xb_ref.dtype)
            o_ref[...] = jnp.zeros(o_ref.shape, o_ref.dtype)

        @pl.when(tv_ref[i] > half)
        def _():
            swiglu_rows(2 * half)

        @pl.when(tv_ref[i] <= half)
        def _():
            swiglu_rows(half)

    @pl.when((i >= nt_ref[0]) & (k == 0))
    def _():
        o_ref[...] = jnp.zeros(o_ref.shape, o_ref.dtype)


def _experts(xs, tile_expert, n_tiles, tile_valid, wg, wu, wd, *, tme, tf=256):
    NS, D = xs.shape
    nk = wg.shape[2] // tf
    nt = NS // tme

    def row_map(i, k, te, n, tv):
        return (jnp.minimum(i, n[0] - 1), 0)

    def chunk(i, k, n):
        return jnp.where(i < n[0], k, nk - 1)

    grid_spec = pltpu.PrefetchScalarGridSpec(
        num_scalar_prefetch=3,
        grid=(nt, nk),
        in_specs=[
            pl.BlockSpec((tme, D), row_map),
            pl.BlockSpec((1, D, tf), lambda i, k, te, n, tv: (te[i], 0, chunk(i, k, n))),
            pl.BlockSpec((1, D, tf), lambda i, k, te, n, tv: (te[i], 0, chunk(i, k, n))),
            pl.BlockSpec((1, tf, D), lambda i, k, te, n, tv: (te[i], chunk(i, k, n), 0)),
        ],
        out_specs=pl.BlockSpec((tme, D), lambda i, k, te, n, tv: (i, 0)),
        scratch_shapes=[pltpu.VMEM((tme, D), BF16)],
    )
    return pl.pallas_call(
        _experts_kernel,
        grid_spec=grid_spec,
        out_shape=jax.ShapeDtypeStruct((NS, D), F32),
        compiler_params=_cparams(("arbitrary", "arbitrary")),
        name="experts",
    )(tile_expert, n_tiles, tile_valid, xs, wg, wu, wd)


def _combine_kernel(pos_ref, nxt_ref, x_ref, pw_ref, g_ref, y_hbm, o_ref, lo_ref, hi_ref, sem, *, tc):
    i = pl.program_id(0)
    slot = i % 2

    def copies(p_ref, s, r):
        return (_row_copy(y_hbm, p_ref[0, 0, 2 * r], lo_ref.at[s], r, sem.at[s]),
                _row_copy(y_hbm, p_ref[0, 0, 2 * r + 1], hi_ref.at[s], r, sem.at[s]))

    def issue(p_ref, s):
        def body(r, c):
            for cp in copies(p_ref, s, r):
                cp.start()
            return c
        lax.fori_loop(0, tc, body, 0, unroll=8)

    @pl.when(i == 0)
    def _():
        issue(pos_ref, 0)

    @pl.when(i + 1 < pl.num_programs(0))
    def _():
        issue(nxt_ref, 1 - slot)

    def wait(r, c):
        for cp in copies(pos_ref, slot, r):
            cp.wait()
        return c

    lax.fori_loop(0, tc, wait, 0, unroll=8)
    pw = pw_ref[...]
    y = x_ref[...] + pw[:, 0:1] * lo_ref[slot] + pw[:, 1:2] * hi_ref[slot]
    o_ref[...] = _rms(y, g_ref[...])


def _combine(x, pw, g, ys, pos2, *, tc=256):
    T, D = x.shape
    nblk = T // tc
    pos_blk = pos2.reshape(nblk, 1, 2 * tc)
    return pl.pallas_call(
        functools.partial(_combine_kernel, tc=tc),
        grid=(nblk,),
        in_specs=[
            pl.BlockSpec((1, 1, 2 * tc), lambda i: (i, 0, 0), memory_space=pltpu.SMEM),
            pl.BlockSpec((1, 1, 2 * tc), lambda i: (jnp.minimum(i + 1, nblk - 1), 0, 0), memory_space=pltpu.SMEM),
            pl.BlockSpec((tc, D), lambda i: (i, 0)),
            pl.BlockSpec((tc, LANES), lambda i: (i, 0)),
            pl.BlockSpec((1, D), lambda i: (0, 0)),
            pl.BlockSpec(memory_space=pl.ANY),
        ],
        out_specs=pl.BlockSpec((tc, D), lambda i: (i, 0)),
        out_shape=jax.ShapeDtypeStruct((T, D), F32),
        scratch_shapes=[
            pltpu.VMEM((2, tc, D), F32),
            pltpu.VMEM((2, tc, D), F32),
            pltpu.SemaphoreType.DMA((2,)),
        ],
        compiler_params=_cparams(("arbitrary",)),
        name="combine",
    )(pos_blk, pos_blk, x, pw, g, ys)


def _rope_tables(seq):
    pos = jnp.arange(seq, dtype=F32)
    inv_freq = ROPE_THETA ** (-jnp.arange(0, QK_ROPE, 2, dtype=F32) / QK_ROPE)
    ang = pos[:, None] * inv_freq[None, :]
    cos, sin = jnp.cos(ang), jnp.sin(ang)
    z32 = jnp.zeros_like(cos)
    z64 = jnp.zeros((seq, LANES - QK_ROPE), F32)
    cos_f = jnp.concatenate([cos, cos, z64], axis=1)
    sin_a = jnp.concatenate([-sin, z32, z64], axis=1)
    sin_b = jnp.concatenate([z32, sin, z64], axis=1)
    return cos_f, sin_a, sin_b


def kernel(x, norm_mix0, w_in, conv_w, q_norm, w_uq, kv_norm, w_ukv, w_out, norm_ffn0, ffn_w_gate,
           ffn_w_up, ffn_w_down, norm_mix1, pool_w, pool_scale, norm_ffn1, router_w, moe_w_gate,
           moe_w_up, moe_w_down, final_norm):
    B, S, D = x.shape
    T = B * S
    H = MLA_HEADS
    tme = 1024
    n_slots = 2 * T + N_EXPERTS * tme

    xf = x.reshape(T, D)

    w_in_p = jnp.pad(w_in[0], ((0, 0), (0, IN_COLS_PAD - w_in.shape[2]))).astype(BF16)
    wq = w_uq[0].reshape(Q_LORA, H, QK_NOPE + QK_ROPE)
    wq = jnp.pad(wq, ((0, 0), (0, 0), (0, HEAD_PAD - QK_NOPE - QK_ROPE))).reshape(Q_LORA, H * HEAD_PAD).astype(BF16)
    wkv = w_ukv[0].reshape(KV_LORA, H, QK_NOPE + V_HEAD)
    wk = wkv[:, :, :QK_NOPE].reshape(KV_LORA, H * QK_NOPE).astype(BF16)
    wv = wkv[:, :, QK_NOPE:].reshape(KV_LORA, H * V_HEAD).astype(BF16)
    wo_a = w_out[0, :CONV_DIM].astype(BF16)
    wo_b = w_out[0, CONV_DIM:].astype(BF16)
    rw = jnp.pad(router_w[0], ((0, 0), (0, LANES - N_EXPERTS)))
    cos_f, sin_a, sin_b = _rope_tables(S)

    proj = _norm_inproj(xf, norm_mix0, w_in_p)
    y_a = _gated_conv(proj, conv_w[0], seq=S)
    q, k, v = _qkv_proj(proj, q_norm, kv_norm, wq, wk, wv, cos_f, sin_a, sin_b, seq=S)
    y_b = _flash_attention(q, k, v, batch=B, seq=S)
    x1 = _out_proj(y_a, y_b, wo_a, wo_b, xf)
    x2 = _dense_ffn(x1, norm_ffn0, ffn_w_gate[0].astype(BF16), ffn_w_up[0].astype(BF16), ffn_w_down[0].astype(BF16))

    x3, h3, gates, sel = _pool_router(x2, norm_mix1, pool_w[0].astype(BF16), pool_scale, norm_ffn1, rw, seq=S)
    pos, pw, tiles, meta = _rank_slots(sel, gates, tme=tme)
    pos2 = pos[:, :2]
    nt = n_slots // tme
    n_tiles = tiles[0:1, 1]
    xs = _dispatch(h3, pos2, meta[0, :N_EXPERTS], meta[1, :N_EXPERTS], meta[2, :N_EXPERTS], n_tiles,
                   n_slots=n_slots, tme=tme)
    ys = _experts(xs, tiles[:nt, 0], n_tiles, tiles[:nt, 2], moe_w_gate[0], moe_w_up[0], moe_w_down[0], tme=tme)
    out = _combine(x3, pw, final_norm.reshape(1, D), ys, pos2)
    return out.reshape(B, S, D)
```
